```python
import math
import jax
import jax.numpy as jnp
from jax import lax
import numpy as np

D_MODEL = 4096
BATCH = 2
SEQ = 8192
DEPTH = 2

GRID_W = 64
CTX_LEN = 256
EPS = 1e-6
LN_EPS = 1e-5
N_BRANCH = 3
N_MOD = 6
FOURIER_GROUPS = 4
FOURIER_GROUP_DIM = D_MODEL // 16
FOURIER_DIM = FOURIER_GROUPS * FOURIER_GROUP_DIM
CONV_DIM = D_MODEL // 4
CONV_WIDTH = 31
ATTN_HEADS = 8
ATTN_HEAD_DIM = 128
ATTN_V_DIM = 2 * ATTN_HEAD_DIM
ATTN_QK_DIM = ATTN_HEADS * 2 * ATTN_HEAD_DIM
ATTN_DIM = ATTN_HEADS * ATTN_V_DIM
Q_BLOCK = 128
ROPE_BASE = 10000.0
ROPE_FREQS = ATTN_HEAD_DIM // 4
IN_SPLITS = [FOURIER_DIM, FOURIER_DIM + CONV_DIM, FOURIER_DIM + 2 * CONV_DIM, FOURIER_DIM + 2 * CONV_DIM + ATTN_QK_DIM, FOURIER_DIM + 2 * CONV_DIM + 2 * ATTN_QK_DIM]
IN_DIM = FOURIER_DIM + 2 * CONV_DIM + 2 * ATTN_QK_DIM + ATTN_DIM
D_FF = 3 * D_MODEL // 2
N_EXPERTS = 8
TOP_K = 2
D_FF_EXPERT = 3 * D_MODEL // 4
N_DENSE = (DEPTH + 1) // 2
N_MOE = DEPTH // 2

kernel_name = 'hybrid_fourier_conformer_diffattn_moe_dit'


def rmsnorm(x, g):
    xf = x.astype(jnp.float32)
    y = xf * lax.rsqrt(jnp.mean(xf * xf, axis=-1, keepdims=True) + EPS)
    return (y * g.astype(jnp.float32)).astype(x.dtype)


def layernorm(x, g, b):
    xf = x.astype(jnp.float32)
    mu = jnp.mean(xf, axis=-1, keepdims=True)
    var = jnp.mean(jnp.square(xf - mu), axis=-1, keepdims=True)
    y = (xf - mu) * lax.rsqrt(var + LN_EPS) * g.astype(jnp.float32) + b.astype(jnp.float32)
    return y.astype(x.dtype)


def ada_params(cond, w_mod, b_mod):
    m = jax.nn.silu(cond) @ w_mod + b_mod
    return m.reshape(cond.shape[:-1] + (N_MOD, D_MODEL))


def modulate(h, shift, scale):
    return h * (1.0 + scale) + shift


def split_kv(u):
    k, v = jnp.split(u, [ATTN_QK_DIM], axis=-1)
    lead = u.shape[:-1]
    return (k.reshape(lead + (ATTN_HEADS, 2, ATTN_HEAD_DIM)),
            v.reshape(lead + (ATTN_HEADS, ATTN_V_DIM)))


def split_in(u):
    f, ca, cb, q, kv = jnp.split(u, IN_SPLITS[:4], axis=-1)
    k, v = split_kv(kv)
    q = q.reshape(u.shape[:-1] + (ATTN_HEADS, 2, ATTN_HEAD_DIM))
    return f, ca, cb, q, k, v


def axial_rope_tables(n):
    rows = n // GRID_W
    row = jnp.repeat(jnp.arange(rows, dtype=jnp.float32), GRID_W)
    col = jnp.tile(jnp.arange(GRID_W, dtype=jnp.float32), rows)
    inv = ROPE_BASE ** (-jnp.arange(ROPE_FREQS, dtype=jnp.float32) / ROPE_FREQS)
    ang = jnp.stack([row[:, None] * inv, col[:, None] * inv], axis=1)
    return jnp.cos(ang), jnp.sin(ang)


def apply_axial_rope(t, cos, sin):
    shp = t.shape
    t = t.reshape(shp[:-1] + (2, 2, ROPE_FREQS))
    t1, t2 = t[..., 0, :], t[..., 1, :]
    cb = cos[None, :, None, None].astype(t.dtype)
    sb = sin[None, :, None, None].astype(t.dtype)
    out = jnp.stack([t1 * cb - t2 * sb, t2 * cb + t1 * sb], axis=-2)
    return out.reshape(shp)


def fourier_mix(f):
    b, n, _ = f.shape
    fg = f.astype(jnp.float32).reshape(b, n, FOURIER_GROUPS, FOURIER_GROUP_DIM)
    y = jnp.fft.fft2(fg, axes=(1, 3), norm='ortho').real
    return y.reshape(b, n, FOURIER_DIM).astype(f.dtype)


def conformer_conv(a, g, w_dw, b_dw, ln_g, ln_b):
    z = a * jax.nn.sigmoid(g)
    pad = CONV_WIDTH // 2
    z = lax.conv_general_dilated(z, w_dw[:, None, :].astype(z.dtype), window_strides=(1,),
                                 padding=[(pad, pad)], dimension_numbers=('NWC', 'WIO', 'NWC'),
                                 feature_group_count=CONV_DIM) + b_dw
    return jax.nn.silu(layernorm(z, ln_g, ln_b))


def diff_attend(q, k, v, lam):
    s = jnp.einsum('bqhmd,bkhmd->bhmqk', q, k).astype(jnp.float32) * (ATTN_HEAD_DIM ** -0.5)
    p = jax.nn.softmax(s, axis=-1)
    a = p[:, :, 0] - lam * p[:, :, 1]
    return jnp.einsum('bhqk,bkhe->bqhe', a.astype(v.dtype), v)


def blocked_diff_attend(q, k, v, lam):
    b, n = q.shape[:2]
    nblk = n // Q_BLOCK
    qb = q.reshape((b, nblk, Q_BLOCK) + q.shape[2:]).swapaxes(0, 1)
    out = lax.map(lambda qi: diff_attend(qi, k, v, lam), qb)
    return out.swapaxes(0, 1).reshape(b, n, ATTN_HEADS, ATTN_V_DIM)


def diff_head_out(o, subln_g, lam_init):
    o = rmsnorm(o, subln_g) * (1.0 - lam_init)
    return o.reshape(o.shape[:2] + (ATTN_DIM,))


def merge_branches(h, four, conv, att, w_gate, b_gate, w_four_out, w_conv_out, w_attn_out, w_o):
    g = jax.nn.sigmoid((h @ w_gate + b_gate).astype(jnp.float32)).astype(h.dtype)
    g = g.reshape(h.shape[:-1] + (N_BRANCH, D_MODEL))
    y = (g[..., 0, :] * (four @ w_four_out) + g[..., 1, :] * (conv @ w_conv_out)
         + g[..., 2, :] * (att @ w_attn_out))
    return y @ w_o


def swiglu(h, w1, w3, w2):
    return (jax.nn.silu(h @ w1) * (h @ w3)) @ w2


def moe_swiglu(h, w_r, b_r, w1e, w3e, w2e):
    logits = (h @ w_r).astype(jnp.float32) + b_r.astype(jnp.float32)
    top_v, top_i = lax.top_k(logits, TOP_K)
    wts = jax.nn.softmax(top_v, axis=-1)
    comb = jnp.sum(jax.nn.one_hot(top_i, N_EXPERTS, dtype=jnp.float32) * wts[..., None], axis=-2)
    comb = comb.astype(h.dtype)
    out = jnp.zeros_like(h)
    for e in range(N_EXPERTS):
        out = out + comb[..., e:e + 1] * swiglu(h, w1e[e], w3e[e], w2e[e])
    return out


def channel_mixer(h, l, w1, w3, w2, w_router, b_router, w1e, w3e, w2e):
    i = l // 2
    if l % 2 == 0:
        return swiglu(h, w1[i], w3[i], w2[i])
    return moe_swiglu(h, w_router[i], b_router[i], w1e[i], w3e[i], w2e[i])


def setup_inputs(seed: int = 0) -> dict:
    key = jax.random.key(seed)
    ks = jax.random.split(key, 40)
    D = D_MODEL

    def nrm(i, shape, scale):
        return jax.random.normal(ks[i], shape, jnp.float32) * scale

    return {
        'x': nrm(0, (BATCH, SEQ, D), 1.0),
        'c': nrm(1, (BATCH, D), 1.0),
        'ctx': nrm(2, (BATCH, CTX_LEN, D), 1.0),
        'c_ctx': nrm(3, (D,), 1.0),
        'norm1_g': 1.0 + nrm(4, (DEPTH, D), 0.02),
        'w_mod': nrm(5, (DEPTH, D, N_MOD * D), 0.5 * D ** -0.5),
        'b_mod': nrm(6, (DEPTH, N_MOD * D), 0.02),
        'w_in': nrm(7, (DEPTH, D, IN_DIM), D ** -0.5),
        'w_gate': nrm(8, (DEPTH, D, N_BRANCH * D), D ** -0.5),
        'b_gate': nrm(9, (DEPTH, N_BRANCH * D), 0.02),
        'w_four_out': nrm(10, (DEPTH, FOURIER_DIM, D), FOURIER_DIM ** -0.5),
        'w_dw': nrm(11, (DEPTH, CONV_WIDTH, CONV_DIM), CONV_WIDTH ** -0.5),
        'b_dw': nrm(12, (DEPTH, CONV_DIM), 0.02),
        'conv_ln_g': 1.0 + nrm(13, (DEPTH, CONV_DIM), 0.02),
        'conv_ln_b': nrm(14, (DEPTH, CONV_DIM), 0.02),
        'w_conv_out': nrm(15, (DEPTH, CONV_DIM, D), CONV_DIM ** -0.5),
        'lam_q1': nrm(16, (DEPTH, ATTN_HEAD_DIM), 0.1),
        'lam_k1': nrm(17, (DEPTH, ATTN_HEAD_DIM), 0.1),
        'lam_q2': nrm(18, (DEPTH, ATTN_HEAD_DIM), 0.1),
        'lam_k2': nrm(19, (DEPTH, ATTN_HEAD_DIM), 0.1),
        'subln_g': 1.0 + nrm(20, (DEPTH, ATTN_V_DIM), 0.02),
        'w_attn_out': nrm(21, (DEPTH, ATTN_DIM, D), ATTN_DIM ** -0.5),
        'w_o': nrm(22, (DEPTH, D, D), D ** -0.5),
        'norm2_g': 1.0 + nrm(23, (DEPTH, D), 0.02),
        'w1': nrm(24, (N_DENSE, D, D_FF), D ** -0.5),
        'w3': nrm(25, (N_DENSE, D, D_FF), D ** -0.5),
        'w2': nrm(26, (N_DENSE, D_FF, D), D_FF ** -0.5),
        'w_router': nrm(27, (N_MOE, D, N_EXPERTS), D ** -0.5),
        'b_router': nrm(28, (N_MOE, N_EXPERTS), 0.01),
        'w1e': nrm(29, (N_MOE, N_EXPERTS, D, D_FF_EXPERT), D ** -0.5),
        'w3e': nrm(30, (N_MOE, N_EXPERTS, D, D_FF_EXPERT), D ** -0.5),
        'w2e': nrm(31, (N_MOE, N_EXPERTS, D_FF_EXPERT, D), D_FF_EXPERT ** -0.5),
        'norm_f_g': 1.0 + nrm(32, (D,), 0.02),
    }


def reference(x, c, ctx, c_ctx, norm1_g, w_mod, b_mod, w_in, w_gate, b_gate, w_four_out, w_dw, b_dw,
              conv_ln_g, conv_ln_b, w_conv_out, lam_q1, lam_k1, lam_q2, lam_k2, subln_g, w_attn_out,
              w_o, norm2_g, w1, w3, w2, w_router, b_router, w1e, w3e, w2e, norm_f_g):
    n = x.shape[1]
    cos, sin = axial_rope_tables(n)
    xc = ctx
    for l in range(DEPTH):
        last = l == DEPTH - 1
        lam_init = 0.8 - 0.6 * math.exp(-0.3 * l)
        lam = (jnp.exp(jnp.sum(lam_q1[l].astype(jnp.float32) * lam_k1[l].astype(jnp.float32)))
               - jnp.exp(jnp.sum(lam_q2[l].astype(jnp.float32) * lam_k2[l].astype(jnp.float32)))
               + lam_init)
        m = ada_params(c, w_mod[l], b_mod[l])[:, None]
        mc = ada_params(c_ctx, w_mod[l], b_mod[l])

        hc = modulate(rmsnorm(xc, norm1_g[l]), mc[..., 0, :], mc[..., 1, :])
        if last:
            kc, vc = split_kv(hc @ w_in[l][:, IN_SPLITS[3]:])
        else:
            fc, cac, cbc, qc, kc, vc = split_in(hc @ w_in[l])

        h = modulate(rmsnorm(x, norm1_g[l]), m[..., 0, :], m[..., 1, :])
        f, ca, cb, q, k, v = split_in(h @ w_in[l])
        q = apply_axial_rope(q, cos, sin)
        k = apply_axial_rope(k, cos, sin)
        four = fourier_mix(f)
        conv = conformer_conv(ca, cb, w_dw[l], b_dw[l], conv_ln_g[l], conv_ln_b[l])
        att = blocked_diff_attend(q, jnp.concatenate([k, kc], axis=1),
                                  jnp.concatenate([v, vc], axis=1), lam)
        att = diff_head_out(att, subln_g[l], lam_init)
        y = merge_branches(h, four, conv, att, w_gate[l], b_gate[l], w_four_out[l],
                           w_conv_out[l], w_attn_out[l], w_o[l])
        x = x + m[..., 2, :] * y

        if not last:
            four_c = fourier_mix(fc)
            conv_c = conformer_conv(cac, cbc, w_dw[l], b_dw[l], conv_ln_g[l], conv_ln_b[l])
            att_c = diff_head_out(diff_attend(qc, kc, vc, lam), subln_g[l], lam_init)
            yc = merge_branches(hc, four_c, conv_c, att_c, w_gate[l], b_gate[l], w_four_out[l],
                                w_conv_out[l], w_attn_out[l], w_o[l])
            xc = xc + mc[..., 2, :] * yc

        h2 = modulate(rmsnorm(x, norm2_g[l]), m[..., 3, :], m[..., 4, :])
        x = x + m[..., 5, :] * channel_mixer(h2, l, w1, w3, w2, w_router, b_router, w1e, w3e, w2e)
        if not last:
            h2c = modulate(rmsnorm(xc, norm2_g[l]), mc[..., 3, :], mc[..., 4, :])
            xc = xc + mc[..., 5, :] * channel_mixer(h2c, l, w1, w3, w2, w_router, b_router, w1e, w3e, w2e)
    return rmsnorm(x, norm_f_g)
```

```python
import functools
import math

import numpy as np
import jax
import jax.numpy as jnp
from jax import lax
from jax.experimental import pallas as pl
from jax.experimental.pallas import tpu as pltpu

F32 = jnp.float32
BF16 = jnp.bfloat16

GRID_W = 64
EPS = 1e-6
LN_EPS = 1e-5
N_MOD = 6
FOURIER_GROUPS = 4
CONV_WIDTH = 31
CONV_PAD = CONV_WIDTH // 2
ATTN_HEADS = 8
ATTN_HEAD_DIM = 128
ATTN_V_DIM = 2 * ATTN_HEAD_DIM
ATTN_QK_DIM = ATTN_HEADS * 2 * ATTN_HEAD_DIM
ATTN_DIM = ATTN_HEADS * ATTN_V_DIM
ROPE_BASE = 10000.0
ROPE_FREQS = ATTN_HEAD_DIM // 4
N_EXPERTS = 8
TOP_K = 2

LANES = 128
BF16_SUBLANES = 16
V7X_VMEM_LIMIT_BYTES = 56 * 1024 * 1024
FFT_INNER = 128
COND_ROWS = 8
HALO_ROWS = BF16_SUBLANES


def _tile(n, pref, align):
    t = (min(pref, n) // align) * align
    while t > align and n % t:
        t -= align
    assert t >= align and n % t == 0, (n, pref, align)
    return t


def _params(semantics):
    return pltpu.CompilerParams(dimension_semantics=semantics,
                                vmem_limit_bytes=V7X_VMEM_LIMIT_BYTES)


def _dot(a, b):
    return jnp.dot(a, b, preferred_element_type=F32)


def _ada_kernel(c_ref, w_ref, b_ref, o_ref):
    c = c_ref[...]
    s = (c * jax.nn.sigmoid(c)).astype(BF16)
    o_ref[0] = _dot(s, w_ref[0].astype(BF16)) + b_ref[0]


def ada_params_all(cond, w_mod, b_mod):
    depth, d, n = w_mod.shape
    bn = _tile(n, 512, LANES)
    return pl.pallas_call(
        _ada_kernel,
        out_shape=jax.ShapeDtypeStruct((depth, COND_ROWS, n), F32),
        grid=(depth, n // bn),
        in_specs=[pl.BlockSpec((COND_ROWS, d), lambda l, j: (0, 0)),
                  pl.BlockSpec((1, d, bn), lambda l, j: (l, 0, j)),
                  pl.BlockSpec((1, 1, bn), lambda l, j: (l, 0, j))],
        out_specs=pl.BlockSpec((1, COND_ROWS, bn), lambda l, j: (l, 0, j)),
        compiler_params=_params(("parallel", "parallel")),
        name="ada_params",
    )(cond, w_mod, b_mod.reshape(depth, 1, n))


def _mod_row(i, blocks_per_batch, n_batch):
    return jnp.minimum(i // blocks_per_batch, n_batch)


def _rms(x, g):
    ms = jnp.mean(x * x, axis=-1, keepdims=True)
    return x * lax.rsqrt(ms + EPS) * g


def _normmod_kernel(x_ref, g_ref, sh_ref, sc_ref, o_ref, *, blocks_per_batch, n_batch):
    r = _mod_row(pl.program_id(0), blocks_per_batch, n_batch)
    y = _rms(x_ref[...], g_ref[...])
    h = y * (1.0 + sc_ref[pl.ds(r, 1), :]) + sh_ref[pl.ds(r, 1), :]
    o_ref[...] = h.astype(o_ref.dtype)


def _normmod_router_kernel(x_ref, g_ref, sh_ref, sc_ref, wr_ref, br_ref, o_ref, comb_ref, *,
                           blocks_per_batch, n_batch):
    r = _mod_row(pl.program_id(0), blocks_per_batch, n_batch)
    y = _rms(x_ref[...], g_ref[...])
    h = y * (1.0 + sc_ref[pl.ds(r, 1), :]) + sh_ref[pl.ds(r, 1), :]
    o_ref[...] = h.astype(o_ref.dtype)
    logits = jnp.dot(h, wr_ref[...], preferred_element_type=F32,
                     precision=lax.Precision.HIGHEST) + br_ref[...]
    lane = lax.broadcasted_iota(jnp.int32, logits.shape, 1)
    v1 = jnp.max(logits, axis=-1, keepdims=True)
    i1 = jnp.min(jnp.where(logits == v1, lane, LANES), axis=-1, keepdims=True)
    rest = jnp.where(lane == i1, -jnp.inf, logits)
    v2 = jnp.max(rest, axis=-1, keepdims=True)
    i2 = jnp.min(jnp.where(rest == v2, lane, LANES), axis=-1, keepdims=True)
    e = jnp.exp(v2 - v1)
    w1 = 1.0 / (1.0 + e)
    comb_ref[...] = jnp.where(lane == i1, w1, 0.0) + jnp.where(lane == i2, e * w1, 0.0)


def _final_norm_kernel(x_ref, g_ref, o_ref):
    o_ref[...] = _rms(x_ref[...], g_ref[...])


def norm_modulate(xa, g, mod_l, which, rows, seq_rows, n_batch, router=None):
    d = xa.shape[1]
    bm = _tile(math.gcd(rows, seq_rows), 256, BF16_SUBLANES)
    kw = dict(blocks_per_batch=seq_rows // bm, n_batch=n_batch)
    in_specs = [pl.BlockSpec((bm, d), lambda i: (i, 0)),
                pl.BlockSpec((1, d), lambda i: (0, 0)),
                pl.BlockSpec((COND_ROWS, d), lambda i: (0, which)),
                pl.BlockSpec((COND_ROWS, d), lambda i: (0, which + 1))]
    args = [xa, g.reshape(1, d), mod_l, mod_l]
    h_shape = jax.ShapeDtypeStruct((rows, d), BF16)
    h_spec = pl.BlockSpec((bm, d), lambda i: (i, 0))
    if router is None:
        return pl.pallas_call(
            functools.partial(_normmod_kernel, **kw), out_shape=h_shape, grid=(rows // bm,),
            in_specs=in_specs, out_specs=h_spec, compiler_params=_params(("parallel",)),
            name="norm_modulate")(*args)
    w_r, b_r = router
    ne = w_r.shape[1]
    w_pad = jnp.zeros((d, LANES), F32).at[:, :ne].set(w_r)
    b_pad = jnp.full((1, LANES), -jnp.inf, F32).at[0, :ne].set(b_r)
    return pl.pallas_call(
        functools.partial(_normmod_router_kernel, **kw),
        out_shape=(h_shape, jax.ShapeDtypeStruct((rows, LANES), F32)), grid=(rows // bm,),
        in_specs=in_specs + [pl.BlockSpec((d, LANES), lambda i: (0, 0)),
                             pl.BlockSpec((1, LANES), lambda i: (0, 0))],
        out_specs=(h_spec, pl.BlockSpec((bm, LANES), lambda i: (i, 0))),
        compiler_params=_params(("parallel",)), name="norm_modulate_router")(*args, w_pad, b_pad)


def final_norm(xa, g, rows):
    d = xa.shape[1]
    bm = _tile(rows, 256, 8)
    return pl.pallas_call(
        _final_norm_kernel, out_shape=jax.ShapeDtypeStruct((rows, d), F32), grid=(rows // bm,),
        in_specs=[pl.BlockSpec((bm, d), lambda i: (i, 0)), pl.BlockSpec((1, d), lambda i: (0, 0))],
        out_specs=pl.BlockSpec((bm, d), lambda i: (i, 0)),
        compiler_params=_params(("parallel",)), name="final_norm")(xa, g.reshape(1, d))


def _mm_plain_kernel(x_ref, w_ref, o_ref):
    o_ref[...] = _dot(x_ref[...], w_ref[...]).astype(o_ref.dtype)


def _mm_rope_kernel(x_ref, w_ref, c_ref, s1_ref, s2_ref, o_ref, *, rope_cols, q_cols, lat_rows,
                    scale):
    j, i = pl.program_id(0), pl.program_id(1)
    acc = _dot(x_ref[...], w_ref[...])
    acc = acc * jnp.where(j < q_cols, scale, 1.0)
    is_rope = jnp.logical_and(j < rope_cols, i < lat_rows)

    @pl.when(is_rope)
    def _():
        c, s1, s2 = c_ref[...], s1_ref[...], s2_ref[...]
        for g in range(acc.shape[1] // LANES):
            t = acc[:, g * LANES:(g + 1) * LANES]
            r = (t * c + pltpu.roll(t, ROPE_FREQS, 1) * s1
                 + pltpu.roll(t, LANES - ROPE_FREQS, 1) * s2)
            o_ref[:, g * LANES:(g + 1) * LANES] = r.astype(o_ref.dtype)

    @pl.when(jnp.logical_not(is_rope))
    def _():
        o_ref[...] = acc.astype(o_ref.dtype)


def _mm_res_kernel(x_ref, w_ref, res_ref, gate_ref, o_ref, *, blocks_per_batch, n_batch):
    r = _mod_row(pl.program_id(1), blocks_per_batch, n_batch)
    o_ref[...] = res_ref[...] + gate_ref[pl.ds(r, 1), :] * _dot(x_ref[...], w_ref[...])


def _mm_swiglu_kernel(x_ref, w1_ref, w3_ref, o_ref):
    x = x_ref[...]
    a = _dot(x, w1_ref[...])
    o_ref[...] = (a * jax.nn.sigmoid(a) * _dot(x, w3_ref[...])).astype(o_ref.dtype)


def _mm_swiglu_comb_kernel(x_ref, w1_ref, w3_ref, comb_ref, o_ref, *, expert):
    x = x_ref[...]
    a = _dot(x, w1_ref[...])
    hm = a * jax.nn.sigmoid(a) * _dot(x, w3_ref[...])
    o_ref[...] = (hm * comb_ref[:, expert:expert + 1]).astype(o_ref.dtype)


def _mm_tiles(rows, seq_rows, n):
    bm = _tile(math.gcd(rows, seq_rows), 512, BF16_SUBLANES)
    bn = _tile(n, 512, LANES)
    return bm, bn


def mm_plain(x, w, col_off, n, rows, seq_rows):
    k = x.shape[1]
    bm, bn = _mm_tiles(rows, seq_rows, math.gcd(n, col_off) if col_off else n)
    off = col_off // bn
    return pl.pallas_call(
        _mm_plain_kernel, out_shape=jax.ShapeDtypeStruct((rows, n), BF16),
        grid=(n // bn, rows // bm),
        in_specs=[pl.BlockSpec((bm, k), lambda j, i: (i, 0)),
                  pl.BlockSpec((k, bn), lambda j, i: (0, off + j))],
        out_specs=pl.BlockSpec((bm, bn), lambda j, i: (i, j)),
        compiler_params=_params(("parallel", "parallel")), name="mm_plain")(x, w)


def mm_qkv_rope(x, w, col_off, rows, lat_rows, seq, tables):
    k = x.shape[1]
    n = 2 * ATTN_QK_DIM + ATTN_DIM
    bm = _tile(math.gcd(rows, seq), 512, BF16_SUBLANES)
    bn = _tile(math.gcd(ATTN_QK_DIM, col_off), 512, LANES)
    off = col_off // bn
    seq_blocks = seq // bm
    tab_spec = pl.BlockSpec((bm, LANES), lambda j, i: (i % seq_blocks, 0))
    kern = functools.partial(_mm_rope_kernel, rope_cols=2 * ATTN_QK_DIM // bn,
                             q_cols=ATTN_QK_DIM // bn, lat_rows=lat_rows // bm,
                             scale=ATTN_HEAD_DIM ** -0.5)
    return pl.pallas_call(
        kern, out_shape=jax.ShapeDtypeStruct((rows, n), BF16), grid=(n // bn, rows // bm),
        in_specs=[pl.BlockSpec((bm, k), lambda j, i: (i, 0)),
                  pl.BlockSpec((k, bn), lambda j, i: (0, off + j)),
                  tab_spec, tab_spec, tab_spec],
        out_specs=pl.BlockSpec((bm, bn), lambda j, i: (i, j)),
        compiler_params=_params(("parallel", "parallel")), name="mm_qkv_rope")(x, w, *tables)


def mm_residual(x, w, res, mod_l, which, rows, seq_rows, n_batch):
    k, n = w.shape
    bm, bn = _mm_tiles(rows, seq_rows, n)
    gate_blk = which * (n // bn)
    kern = functools.partial(_mm_res_kernel, blocks_per_batch=seq_rows // bm, n_batch=n_batch)
    return pl.pallas_call(
        kern, out_shape=jax.ShapeDtypeStruct((rows, n), F32), grid=(n // bn, rows // bm),
        in_specs=[pl.BlockSpec((bm, k), lambda j, i: (i, 0)),
                  pl.BlockSpec((k, bn), lambda j, i: (0, j)),
                  pl.BlockSpec((bm, bn), lambda j, i: (i, j)),
                  pl.BlockSpec((COND_ROWS, bn), lambda j, i: (0, gate_blk + j))],
        out_specs=pl.BlockSpec((bm, bn), lambda j, i: (i, j)),
        compiler_params=_params(("parallel", "parallel")), name="mm_residual")(x, w, res, mod_l)


def mm_swiglu(x, w1, w3, rows, seq_rows, comb=None, expert=0):
    k, n = w1.shape
    bm, bn = _mm_tiles(rows, seq_rows, n)
    in_specs = [pl.BlockSpec((bm, k), lambda j, i: (i, 0)),
                pl.BlockSpec((k, bn), lambda j, i: (0, j)),
                pl.BlockSpec((k, bn), lambda j, i: (0, j))]
    args = [x, w1, w3]
    kern = _mm_swiglu_kernel
    if comb is not None:
        in_specs.append(pl.BlockSpec((bm, LANES), lambda j, i: (i, 0)))
        args.append(comb)
        kern = functools.partial(_mm_swiglu_comb_kernel, expert=expert)
    return pl.pallas_call(
        kern, out_shape=jax.ShapeDtypeStruct((rows, n), BF16), grid=(n // bn, rows // bm),
        in_specs=in_specs, out_specs=pl.BlockSpec((bm, bn), lambda j, i: (i, j)),
        compiler_params=_params(("parallel", "parallel")), name="mm_swiglu")(*args)


def _merge_kernel(h_ref, f_ref, c_ref, a_ref, wg0, wg1, wg2, bg0, bg1, bg2, wf, wc, wa, o_ref):
    h = h_ref[...]

    def branch(wg, bg, x_ref, w):
        gate = jax.nn.sigmoid(_dot(h, wg[...]) + bg[...])
        return gate * _dot(x_ref[...], w[...])

    y = branch(wg0, bg0, f_ref, wf) + branch(wg1, bg1, c_ref, wc) + branch(wg2, bg2, a_ref, wa)
    o_ref[...] = y.astype(o_ref.dtype)


def merge_branches(h, four, conv, att, w_gate, b_gate, w_four_out, w_conv_out, w_attn_out, rows,
                   seq_rows):
    d = h.shape[1]
    bm = _tile(math.gcd(rows, seq_rows), 512, BF16_SUBLANES)
    bn = _tile(d, 512, LANES)
    nb = d // bn
    b2 = b_gate.reshape(1, 3 * d)

    def row_spec(width):
        return pl.BlockSpec((bm, width), lambda j, i: (i, 0))

    def col_spec(kdim, blk_off):
        return pl.BlockSpec((kdim, bn), lambda j, i: (0, blk_off + j))

    in_specs = ([row_spec(d), row_spec(four.shape[1]), row_spec(conv.shape[1]),
                 row_spec(att.shape[1])]
                + [col_spec(d, r * nb) for r in range(3)]
                + [col_spec(1, r * nb) for r in range(3)]
                + [col_spec(four.shape[1], 0), col_spec(conv.shape[1], 0),
                   col_spec(att.shape[1], 0)])
    return pl.pallas_call(
        _merge_kernel, out_shape=jax.ShapeDtypeStruct((rows, d), BF16), grid=(nb, rows // bm),
        in_specs=in_specs, out_specs=pl.BlockSpec((bm, bn), lambda j, i: (i, j)),
        compiler_params=_params(("parallel", "parallel")), name="merge_branches",
    )(h, four, conv, att, w_gate, w_gate, w_gate, b2, b2, b2, w_four_out, w_conv_out, w_attn_out)


def _fourier_tables(n_pos, group_dim, n2):
    n1 = n_pos // n2
    a = np.arange(n1)
    ang1 = 2.0 * np.pi * np.outer(a, a) / n1
    w1 = np.concatenate([np.cos(ang1), -np.sin(ang1)], axis=0)
    k = (np.arange(n1)[:, None] + n1 * np.arange(n2)[None, :])[:, :, None]
    ang2 = 2.0 * np.pi * ((k * np.arange(n2)[None, None, :]) % n_pos) / n_pos
    gc, gs = np.cos(ang2), np.sin(ang2)
    g = np.concatenate([np.concatenate([gc, gs], axis=2),
                        np.concatenate([-gs, gc], axis=2)], axis=1)
    c = np.arange(group_dim)
    angc = 2.0 * np.pi * np.outer(c, c) / group_dim
    scale = 1.0 / math.sqrt(n_pos * group_dim)
    as_bf16 = lambda t: jnp.asarray(t, dtype=F32).astype(BF16)
    return as_bf16(w1), as_bf16(g), as_bf16(np.cos(angc)), as_bf16(np.sin(angc)), scale


def _four1_kernel(w_ref, x_ref, o_ref):
    o_ref[0] = _dot(w_ref[...], x_ref[...]).astype(o_ref.dtype)


def _four2_kernel(*refs, scale, group_dim, complex_in):
    if complex_in:
        g_ref, zr_ref, zi_ref, cc_ref, sc_ref, o_ref = refs
        z = jnp.concatenate([zr_ref[0], zi_ref[0]], axis=0)
        g = g_ref[0]
    else:
        g_ref, zr_ref, cc_ref, sc_ref, o_ref = refs
        z = zr_ref[...]
        g = g_ref[0][:, :z.shape[0]]
    p = _dot(g, z)
    half = p.shape[0] // 2
    pr, pi = p[:half].astype(BF16), p[half:].astype(BF16)
    cc, sc = cc_ref[...], sc_ref[...]
    for grp in range(o_ref.shape[-1] // group_dim):
        sl = slice(grp * group_dim, (grp + 1) * group_dim)
        o = _dot(pr[:, sl], cc) + _dot(pi[:, sl], sc)
        o_ref[:, sl] = (o * scale).astype(o_ref.dtype)


def _four2_ctx_kernel(g_ref, zr_ref, cc_ref, sc_ref, alias_ref, o_ref, **kw):
    del alias_ref
    _four2_kernel(g_ref, zr_ref, cc_ref, sc_ref, o_ref, **kw)


def fourier_latent(f, n_batch, seq, total_rows):
    fdim = f.shape[1]
    gd = fdim // FOURIER_GROUPS
    n2 = FFT_INNER
    n1 = seq // n2
    w1, g, cc, sc, scale = _fourier_tables(seq, gd, n2)
    wide = n2 * fdim
    bn = _tile(wide, 8192, LANES)
    z = pl.pallas_call(
        _four1_kernel, out_shape=jax.ShapeDtypeStruct((n_batch, 2 * n1, wide), BF16),
        grid=(n_batch, wide // bn),
        in_specs=[pl.BlockSpec((2 * n1, n1), lambda b, j: (0, 0)),
                  pl.BlockSpec((n1, bn), lambda b, j: (b, j))],
        out_specs=pl.BlockSpec((1, 2 * n1, bn), lambda b, j: (b, 0, j)),
        compiler_params=_params(("parallel", "parallel")), name="fourier_stage1",
    )(w1, f.reshape(f.shape[0] // n2, wide))
    z = z.reshape(n_batch * 2 * n1, n2, fdim)
    kern = functools.partial(_four2_kernel, scale=scale, group_dim=gd, complex_in=True)
    out = pl.pallas_call(
        kern, out_shape=jax.ShapeDtypeStruct((total_rows // n1, n1 * fdim), BF16),
        grid=(n_batch, n1),
        in_specs=[pl.BlockSpec((1, 2 * n2, 2 * n2), lambda b, k: (k, 0, 0)),
                  pl.BlockSpec((1, n2, fdim), lambda b, k: (b * 2 * n1 + k, 0, 0)),
                  pl.BlockSpec((1, n2, fdim), lambda b, k: (b * 2 * n1 + n1 + k, 0, 0)),
                  pl.BlockSpec((gd, gd), lambda b, k: (0, 0)),
                  pl.BlockSpec((gd, gd), lambda b, k: (0, 0))],
        out_specs=pl.BlockSpec((n2, fdim), lambda b, k: (b, k)),
        compiler_params=_params(("parallel", "parallel")), name="fourier_stage2",
    )(g, z, z, cc, sc)
    return out.reshape(total_rows, fdim)


def fourier_context(f, four, n_batch, ctx_len, lat_rows):
    fdim = f.shape[1]
    gd = fdim // FOURIER_GROUPS
    n2 = ctx_len
    _, g, cc, sc, scale = _fourier_tables(ctx_len, gd, n2)
    first = lat_rows // ctx_len
    kern = functools.partial(_four2_ctx_kernel, scale=scale, group_dim=gd, complex_in=False)
    return pl.pallas_call(
        kern, out_shape=jax.ShapeDtypeStruct(four.shape, BF16), grid=(n_batch,),
        in_specs=[pl.BlockSpec((1, 2 * n2, 2 * n2), lambda b: (0, 0, 0)),
                  pl.BlockSpec((ctx_len, fdim), lambda b: (first + b, 0)),
                  pl.BlockSpec((gd, gd), lambda b: (0, 0)),
                  pl.BlockSpec((gd, gd), lambda b: (0, 0)),
                  pl.BlockSpec(memory_space=pl.ANY)],
        out_specs=pl.BlockSpec((ctx_len, fdim), lambda b: (first + b, 0)),
        input_output_aliases={4: 0},
        compiler_params=_params(("parallel",)), name="fourier_context",
    )(g, f, cc, sc, four)


def _conv_kernel(prev_ref, cur_ref, next_ref, w_ref, b_ref, g_ref, bb_ref, o_ref, z_ref, *,
                 starts, ends, chunk):
    i = pl.program_id(0)
    ts, c = o_ref.shape

    def glu(ref):
        v = ref[...].astype(F32)
        return v[:, :c] * jax.nn.sigmoid(v[:, c:])

    def any_eq(vals):
        hit = i == vals[0]
        for v in vals[1:]:
            hit = jnp.logical_or(hit, i == v)
        return hit

    z_ref[0:HALO_ROWS, :] = jnp.where(any_eq(starts), 0.0, glu(prev_ref))
    z_ref[HALO_ROWS:HALO_ROWS + ts, :] = glu(cur_ref)
    z_ref[HALO_ROWS + ts:, :] = jnp.where(any_eq(ends), 0.0, glu(next_ref))
    bias, gamma, beta = b_ref[...], g_ref[...], bb_ref[...]
    for r0 in range(0, ts, chunk):
        acc = jnp.zeros((chunk, c), F32)
        for t in range(CONV_WIDTH):
            lo = r0 + t + HALO_ROWS - CONV_PAD
            acc = acc + w_ref[t:t + 1, :] * z_ref[lo:lo + chunk, :]
        acc = acc + bias
        mu = jnp.mean(acc, axis=-1, keepdims=True)
        dev = acc - mu
        var = jnp.mean(dev * dev, axis=-1, keepdims=True)
        y = dev * lax.rsqrt(var + LN_EPS) * gamma + beta
        o_ref[r0:r0 + chunk, :] = (y * jax.nn.sigmoid(y)).astype(o_ref.dtype)


def conformer_conv(cab, w_dw, b_dw, ln_g, ln_b, rows, seq_lens):
    c = cab.shape[1] // 2
    ts = _tile(math.gcd(*seq_lens) if len(seq_lens) > 1 else seq_lens[0], 256, HALO_ROWS)
    per = ts // HALO_ROWS
    starts, ends, pos = [], [], 0
    for n in seq_lens:
        starts.append(pos // ts)
        pos += n
        ends.append(pos // ts - 1)
    assert pos == rows
    last_halo = rows // HALO_ROWS - 1
    kern = functools.partial(_conv_kernel, starts=tuple(starts), ends=tuple(ends), chunk=16)
    vec = lambda v: v.reshape(1, c)
    vspec = pl.BlockSpec((1, c), lambda i: (0, 0))
    return pl.pallas_call(
        kern, out_shape=jax.ShapeDtypeStruct((rows, c), BF16), grid=(rows // ts,),
        in_specs=[pl.BlockSpec((HALO_ROWS, 2 * c), lambda i: (jnp.maximum(i * per - 1, 0), 0)),
                  pl.BlockSpec((ts, 2 * c), lambda i: (i, 0)),
                  pl.BlockSpec((HALO_ROWS, 2 * c),
                               lambda i: (jnp.minimum((i + 1) * per, last_halo), 0)),
                  pl.BlockSpec((CONV_WIDTH, c), lambda i: (0, 0)), vspec, vspec, vspec],
        out_specs=pl.BlockSpec((ts, c), lambda i: (i, 0)),
        scratch_shapes=[pltpu.VMEM((ts + 2 * HALO_ROWS, c), F32)],
        compiler_params=_params(("parallel",)), name="conformer_conv",
    )(cab, cab, cab, w_dw, vec(b_dw), vec(ln_g), vec(ln_b))


def _attn_kernel(*refs, lam_init, tk, has_latent):
    if has_latent:
        q_ref, k_ref, v_ref, kc_ref, vc_ref, lam_ref, g_ref, o_ref, m_ref, l_ref, acc_ref = refs
    else:
        q_ref, kc_ref, vc_ref, lam_ref, g_ref, o_ref, m_ref, l_ref, acc_ref = refs
    hd = ATTN_HEAD_DIM
    m_ref[...] = jnp.full(m_ref.shape, -jnp.inf, F32)
    l_ref[...] = jnp.zeros(l_ref.shape, F32)
    acc_ref[...] = jnp.zeros(acc_ref.shape, F32)

    def step(k, v):
        n = k.shape[0]
        for mp in range(2):
            q = q_ref[:, mp * hd:(mp + 1) * hd]
            s = lax.dot_general(q, k[:, mp * hd:(mp + 1) * hd], (((1,), (1,)), ((), ())),
                                preferred_element_type=F32)
            m_prev = m_ref[mp]
            m_new = jnp.maximum(m_prev, jnp.max(s, axis=-1, keepdims=True))
            alpha = jnp.exp(m_prev - m_new)
            p = jnp.exp(s - pltpu.repeat(m_new, n // LANES, axis=1))
            l_ref[mp] = alpha * l_ref[mp] + jnp.sum(p, axis=-1, keepdims=True)
            acc_ref[mp] = (acc_ref[mp] * pltpu.repeat(alpha, ATTN_V_DIM // LANES, axis=1)
                           + _dot(p.astype(BF16), v))
            m_ref[mp] = m_new

    if has_latent:
        def body(c, carry):
            off = pl.multiple_of(c * tk, tk)
            step(k_ref[pl.ds(off, tk), :], v_ref[pl.ds(off, tk), :])
            return carry
        lax.fori_loop(0, k_ref.shape[0] // tk, body, 0)
    step(kc_ref[...], vc_ref[...])

    lam_v = lam_ref[...]
    lam = (jnp.exp(jnp.sum(lam_v[0:1] * lam_v[1:2], axis=-1, keepdims=True))
           - jnp.exp(jnp.sum(lam_v[2:3] * lam_v[3:4], axis=-1, keepdims=True)) + lam_init)
    reps = ATTN_V_DIM // LANES
    o = (acc_ref[0] * pltpu.repeat(1.0 / l_ref[0], reps, axis=1)
         - lam * (acc_ref[1] * pltpu.repeat(1.0 / l_ref[1], reps, axis=1)))
    o = _rms(o, g_ref[...]) * (1.0 - lam_init)
    o_ref[...] = o.astype(o_ref.dtype)


def _attn_scratch(tq):
    return [pltpu.VMEM((2, tq, LANES), F32), pltpu.VMEM((2, tq, LANES), F32),
            pltpu.VMEM((2, tq, ATTN_V_DIM), F32)]


def attention_latent(qkv, lam_vecs, subln_g, lam_init, n_batch, seq, ctx_len, total_rows):
    tq = _tile(seq, 512, BF16_SUBLANES)
    tk = _tile(seq, 512, LANES)
    qb = seq // tq
    kblk = ATTN_QK_DIM // ATTN_V_DIM
    vblk = 2 * ATTN_QK_DIM // ATTN_V_DIM
    first_ctx = n_batch * seq // ctx_len
    kern = functools.partial(_attn_kernel, lam_init=lam_init, tk=tk, has_latent=True)
    w = ATTN_V_DIM
    return pl.pallas_call(
        kern, out_shape=jax.ShapeDtypeStruct((total_rows, ATTN_DIM), BF16),
        grid=(n_batch, ATTN_HEADS, qb),
        in_specs=[pl.BlockSpec((tq, w), lambda b, h, i: (b * qb + i, h)),
                  pl.BlockSpec((seq, w), lambda b, h, i: (b, kblk + h)),
                  pl.BlockSpec((seq, w), lambda b, h, i: (b, vblk + h)),
                  pl.BlockSpec((ctx_len, w), lambda b, h, i: (first_ctx + b, kblk + h)),
                  pl.BlockSpec((ctx_len, w), lambda b, h, i: (first_ctx + b, vblk + h)),
                  pl.BlockSpec((4, ATTN_HEAD_DIM), lambda b, h, i: (0, 0)),
                  pl.BlockSpec((1, w), lambda b, h, i: (0, 0))],
        out_specs=pl.BlockSpec((tq, w), lambda b, h, i: (b * qb + i, h)),
        scratch_shapes=_attn_scratch(tq),
        compiler_params=_params(("parallel", "parallel", "parallel")), name="attention_latent",
    )(qkv, qkv, qkv, qkv, qkv, lam_vecs, subln_g.reshape(1, w))


def _attn_ctx_kernel(q_ref, kc_ref, vc_ref, lam_ref, g_ref, alias_ref, o_ref, m_ref, l_ref,
                     acc_ref, **kw):
    del alias_ref
    _attn_kernel(q_ref, kc_ref, vc_ref, lam_ref, g_ref, o_ref, m_ref, l_ref, acc_ref, **kw)


def attention_context(qkv, att, lam_vecs, subln_g, lam_init, n_batch, ctx_len, lat_rows):
    kblk = ATTN_QK_DIM // ATTN_V_DIM
    vblk = 2 * ATTN_QK_DIM // ATTN_V_DIM
    first = lat_rows // ctx_len
    kern = functools.partial(_attn_ctx_kernel, lam_init=lam_init, tk=ctx_len, has_latent=False)
    w = ATTN_V_DIM
    return pl.pallas_call(
        kern, out_shape=jax.ShapeDtypeStruct(att.shape, BF16), grid=(n_batch, ATTN_HEADS),
        in_specs=[pl.BlockSpec((ctx_len, w), lambda b, h: (first + b, h)),
                  pl.BlockSpec((ctx_len, w), lambda b, h: (first + b, kblk + h)),
                  pl.BlockSpec((ctx_len, w), lambda b, h: (first + b, vblk + h)),
                  pl.BlockSpec((4, ATTN_HEAD_DIM), lambda b, h: (0, 0)),
                  pl.BlockSpec((1, w), lambda b, h: (0, 0)),
                  pl.BlockSpec(memory_space=pl.ANY)],
        out_specs=pl.BlockSpec((ctx_len, w), lambda b, h: (first + b, h)),
        input_output_aliases={5: 0}, scratch_shapes=_attn_scratch(ctx_len),
        compiler_params=_params(("parallel", "parallel")), name="attention_context",
    )(qkv, qkv, qkv, lam_vecs, subln_g.reshape(1, w), att)


def _rope_tables(n):
    pos = np.arange(n)
    inv = ROPE_BASE ** (-np.arange(ROPE_FREQS, dtype=np.float64) / ROPE_FREQS)
    ang = [(pos // GRID_W)[:, None] * inv, (pos % GRID_W)[:, None] * inv]
    zero = np.zeros((n, ROPE_FREQS))
    cos = np.concatenate([np.cos(ang[0])] * 2 + [np.cos(ang[1])] * 2, axis=1)
    s_up = np.concatenate([zero, np.sin(ang[0]), zero, np.sin(ang[1])], axis=1)
    s_dn = np.concatenate([-np.sin(ang[0]), zero, -np.sin(ang[1]), zero], axis=1)
    return tuple(jnp.asarray(t, dtype=F32) for t in (cos, s_up, s_dn))


def kernel(x, c, ctx, c_ctx, norm1_g, w_mod, b_mod, w_in, w_gate, b_gate, w_four_out, w_dw, b_dw,
           conv_ln_g, conv_ln_b, w_conv_out, lam_q1, lam_k1, lam_q2, lam_k2, subln_g, w_attn_out,
           w_o, norm2_g, w1, w3, w2, w_router, b_router, w1e, w3e, w2e, norm_f_g):
    n_batch, seq, d = x.shape
    ctx_len = ctx.shape[1]
    depth = w_in.shape[0]
    lat_rows, ctx_rows = n_batch * seq, n_batch * ctx_len
    all_rows = lat_rows + ctx_rows
    fdim = w_four_out.shape[1]
    cdim = w_conv_out.shape[1]
    assert n_batch < COND_ROWS and ctx_len % HALO_ROWS == 0 and seq % FFT_INNER == 0

    xa = jnp.concatenate([x.reshape(lat_rows, d), ctx.reshape(ctx_rows, d)], axis=0)
    cond = jnp.zeros((COND_ROWS, d), F32).at[:n_batch].set(c).at[n_batch].set(c_ctx)
    mod = ada_params_all(cond, w_mod, b_mod)
    rope = _rope_tables(seq)

    for l in range(depth):
        last = l == depth - 1
        rows = lat_rows if last else all_rows
        seqs = [seq] * n_batch + ([] if last else [ctx_len] * n_batch)
        lam_init = 0.8 - 0.6 * math.exp(-0.3 * l)
        lam_vecs = jnp.stack([lam_q1[l], lam_k1[l], lam_q2[l], lam_k2[l]]).astype(F32)
        w_in_l = w_in[l].astype(BF16)

        h = norm_modulate(xa, norm1_g[l], mod[l], 0, all_rows, seq, n_batch)
        f = mm_plain(h, w_in_l, 0, fdim, rows, seq)
        cab = mm_plain(h, w_in_l, fdim, 2 * cdim, rows, seq)
        qkv = mm_qkv_rope(h, w_in_l, fdim + 2 * cdim, all_rows, lat_rows, seq, rope)
        four = fourier_latent(f, n_batch, seq, rows)
        att = attention_latent(qkv, lam_vecs, subln_g[l], lam_init, n_batch, seq, ctx_len, rows)
        if not last:
            four = fourier_context(f, four, n_batch, ctx_len, lat_rows)
            att = attention_context(qkv, att, lam_vecs, subln_g[l], lam_init, n_batch, ctx_len,
                                    lat_rows)
        conv = conformer_conv(cab, w_dw[l], b_dw[l], conv_ln_g[l], conv_ln_b[l], rows, seqs)
        y = merge_branches(h, four, conv, att, w_gate[l].astype(BF16), b_gate[l],
                           w_four_out[l].astype(BF16), w_conv_out[l].astype(BF16),
                           w_attn_out[l].astype(BF16), rows, seq)
        xa = mm_residual(y, w_o[l].astype(BF16), xa, mod[l], 2, rows, seq, n_batch)

        i = l // 2
        if l % 2 == 0:
            h2 = norm_modulate(xa, norm2_g[l], mod[l], 3, rows, seq, n_batch)
            hm = mm_swiglu(h2, w1[i].astype(BF16), w3[i].astype(BF16), rows, seq)
            xa = mm_residual(hm, w2[i].astype(BF16), xa, mod[l], 5, rows, seq, n_batch)
        else:
            h2, comb = norm_modulate(xa, norm2_g[l], mod[l], 3, rows, seq, n_batch,
                                     router=(w_router[i], b_router[i]))
            for e in range(w1e.shape[1]):
                hm = mm_swiglu(h2, w1e[i, e].astype(BF16), w3e[i, e].astype(BF16), rows, seq,
                               comb=comb, expert=e)
                xa = mm_residual(hm, w2e[i, e].astype(BF16), xa, mod[l], 5, rows, seq, n_batch)

    return final_norm(xa, norm_f_g, lat_rows).reshape(n_batch, seq, d)
```

```python
import functools
import math

import numpy as np
import jax
import jax.numpy as jnp
from jax import lax
from jax.experimental import pallas as pl
from jax.experimental.pallas import tpu as pltpu

F32 = jnp.float32
BF16 = jnp.bfloat16

GRID_W = 64
EPS = 1e-6
LN_EPS = 1e-5
N_MOD = 6
FOURIER_GROUPS = 4
CONV_WIDTH = 31
CONV_PAD = CONV_WIDTH // 2
ATTN_HEADS = 8
ATTN_HEAD_DIM = 128
ATTN_V_DIM = 2 * ATTN_HEAD_DIM
ATTN_QK_DIM = ATTN_HEADS * 2 * ATTN_HEAD_DIM
ATTN_DIM = ATTN_HEADS * ATTN_V_DIM
ROPE_BASE = 10000.0
ROPE_FREQS = ATTN_HEAD_DIM // 4
N_EXPERTS = 8
TOP_K = 2

LANES = 128
BF16_SUBLANES = 16
V7X_VMEM_LIMIT_BYTES = 56 * 1024 * 1024
FFT_INNER = 128
COND_ROWS = 8
HALO_ROWS = BF16_SUBLANES


def _tile(n, pref, align):
    t = (min(pref, n) // align) * align
    while t > align and n % t:
        t -= align
    assert t >= align and n % t == 0, (n, pref, align)
    return t


def _params(semantics):
    return pltpu.CompilerParams(dimension_semantics=semantics,
                                vmem_limit_bytes=V7X_VMEM_LIMIT_BYTES)


def _dot(a, b):
    return jnp.dot(a, b, preferred_element_type=F32)


def _ada_kernel(c_ref, w_ref, b_ref, o_ref):
    c = c_ref[...]
    s = (c * jax.nn.sigmoid(c)).astype(BF16)
    o_ref[0] = _dot(s, w_ref[0].astype(BF16)) + b_ref[0]


def ada_params_all(cond, w_mod, b_mod):
    depth, d, n = w_mod.shape
    bn = _tile(n, 512, LANES)
    return pl.pallas_call(
        _ada_kernel,
        out_shape=jax.ShapeDtypeStruct((depth, COND_ROWS, n), F32),
        grid=(depth, n // bn),
        in_specs=[pl.BlockSpec((COND_ROWS, d), lambda l, j: (0, 0)),
                  pl.BlockSpec((1, d, bn), lambda l, j: (l, 0, j)),
                  pl.BlockSpec((1, 1, bn), lambda l, j: (l, 0, j))],
        out_specs=pl.BlockSpec((1, COND_ROWS, bn), lambda l, j: (l, 0, j)),
        compiler_params=_params(("parallel", "parallel")),
        name="ada_params",
    )(cond, w_mod, b_mod.reshape(depth, 1, n))


def _mod_row(i, blocks_per_batch, n_batch):
    return jnp.minimum(i // blocks_per_batch, n_batch)


def _rms(x, g):
    ms = jnp.mean(x * x, axis=-1, keepdims=True)
    return x * lax.rsqrt(ms + EPS) * g


def _normmod_kernel(x_ref, g_ref, sh_ref, sc_ref, o_ref, *, blocks_per_batch, n_batch):
    r = _mod_row(pl.program_id(0), blocks_per_batch, n_batch)
    y = _rms(x_ref[...], g_ref[...])
    h = y * (1.0 + sc_ref[pl.ds(r, 1), :]) + sh_ref[pl.ds(r, 1), :]
    o_ref[...] = h.astype(o_ref.dtype)


def _normmod_router_kernel(x_ref, g_ref, sh_ref, sc_ref, wr_ref, br_ref, o_ref, route_ref, *,
                           blocks_per_batch, n_batch):
    r = _mod_row(pl.program_id(0), blocks_per_batch, n_batch)
    y = _rms(x_ref[...], g_ref[...])
    h = y * (1.0 + sc_ref[pl.ds(r, 1), :]) + sh_ref[pl.ds(r, 1), :]
    bits = lax.bitcast_convert_type(h.astype(BF16).astype(F32), jnp.uint32)
    half = bits.shape[1] // 2
    o_ref[...] = bits[:, :half] | (bits[:, half:] >> 16)
    logits = jnp.dot(h, wr_ref[...], preferred_element_type=F32,
                     precision=lax.Precision.HIGHEST) + br_ref[...]
    lane = lax.broadcasted_iota(jnp.int32, logits.shape, 1)
    v1 = jnp.max(logits, axis=-1, keepdims=True)
    i1 = jnp.min(jnp.where(logits == v1, lane, LANES), axis=-1, keepdims=True)
    rest = jnp.where(lane == i1, -jnp.inf, logits)
    v2 = jnp.max(rest, axis=-1, keepdims=True)
    i2 = jnp.min(jnp.where(rest == v2, lane, LANES), axis=-1, keepdims=True)
    e = jnp.exp(v2 - v1)
    w1 = 1.0 / (1.0 + e)
    route_ref[...] = (jnp.where(lane == 0, w1, 0.0) + jnp.where(lane == 1, e * w1, 0.0)
                      + jnp.where(lane == 2, i1.astype(F32), 0.0)
                      + jnp.where(lane == 3, i2.astype(F32), 0.0))


def _final_norm_kernel(x_ref, g_ref, o_ref):
    o_ref[...] = _rms(x_ref[...], g_ref[...])


def norm_modulate(xa, g, mod_l, which, rows, seq_rows, n_batch, router=None):
    d = xa.shape[1]
    bm = _tile(math.gcd(rows, seq_rows), 256, BF16_SUBLANES)
    kw = dict(blocks_per_batch=seq_rows // bm, n_batch=n_batch)
    in_specs = [pl.BlockSpec((bm, d), lambda i: (i, 0)),
                pl.BlockSpec((1, d), lambda i: (0, 0)),
                pl.BlockSpec((COND_ROWS, d), lambda i: (0, which)),
                pl.BlockSpec((COND_ROWS, d), lambda i: (0, which + 1))]
    args = [xa, g.reshape(1, d), mod_l, mod_l]
    h_shape = jax.ShapeDtypeStruct((rows, d), BF16)
    h_spec = pl.BlockSpec((bm, d), lambda i: (i, 0))
    if router is None:
        return pl.pallas_call(
            functools.partial(_normmod_kernel, **kw), out_shape=h_shape, grid=(rows // bm,),
            in_specs=in_specs, out_specs=h_spec, compiler_params=_params(("parallel",)),
            name="norm_modulate")(*args)
    w_r, b_r = router
    ne = w_r.shape[1]
    w_pad = jnp.zeros((d, LANES), F32).at[:, :ne].set(w_r)
    b_pad = jnp.full((1, LANES), -jnp.inf, F32).at[0, :ne].set(b_r)
    return pl.pallas_call(
        functools.partial(_normmod_router_kernel, **kw),
        out_shape=(jax.ShapeDtypeStruct((rows, d // 2), jnp.uint32),
                   jax.ShapeDtypeStruct((rows, LANES), F32)), grid=(rows // bm,),
        in_specs=in_specs + [pl.BlockSpec((d, LANES), lambda i: (0, 0)),
                             pl.BlockSpec((1, LANES), lambda i: (0, 0))],
        out_specs=(pl.BlockSpec((bm, d // 2), lambda i: (i, 0)),
                   pl.BlockSpec((bm, LANES), lambda i: (i, 0))),
        compiler_params=_params(("parallel",)), name="norm_modulate_router")(*args, w_pad, b_pad)


def final_norm(xa, g, rows):
    d = xa.shape[1]
    bm = _tile(rows, 256, 8)
    return pl.pallas_call(
        _final_norm_kernel, out_shape=jax.ShapeDtypeStruct((rows, d), F32), grid=(rows // bm,),
        in_specs=[pl.BlockSpec((bm, d), lambda i: (i, 0)), pl.BlockSpec((1, d), lambda i: (0, 0))],
        out_specs=pl.BlockSpec((bm, d), lambda i: (i, 0)),
        compiler_params=_params(("parallel",)), name="final_norm")(xa, g.reshape(1, d))


def _stage_weight(w_ref, wb_ref):
    @pl.when(pl.program_id(1) == 0)
    def _():
        wb_ref[...] = w_ref[0].astype(BF16)


def _wspec(layer, k, bn, col_blk_off=0):
    return pl.BlockSpec((1, k, bn), lambda j, i: (layer, 0, col_blk_off + j))


def _mm_plain_kernel(x_ref, w_ref, o_ref, wb_ref):
    _stage_weight(w_ref, wb_ref)
    o_ref[...] = _dot(x_ref[...], wb_ref[...]).astype(o_ref.dtype)


def _mm_rope_kernel(x_ref, w_ref, c_ref, s1_ref, s2_ref, o_ref, wb_ref, *, rope_cols, q_cols,
                    lat_rows, scale):
    j, i = pl.program_id(0), pl.program_id(1)
    _stage_weight(w_ref, wb_ref)
    acc = _dot(x_ref[...], wb_ref[...])
    acc = acc * jnp.where(j < q_cols, scale, 1.0)
    is_rope = jnp.logical_and(j < rope_cols, i < lat_rows)

    @pl.when(is_rope)
    def _():
        c, s1, s2 = c_ref[...], s1_ref[...], s2_ref[...]
        for g in range(acc.shape[1] // LANES):
            t = acc[:, g * LANES:(g + 1) * LANES]
            r = (t * c + pltpu.roll(t, ROPE_FREQS, 1) * s1
                 + pltpu.roll(t, LANES - ROPE_FREQS, 1) * s2)
            o_ref[:, g * LANES:(g + 1) * LANES] = r.astype(o_ref.dtype)

    @pl.when(jnp.logical_not(is_rope))
    def _():
        o_ref[...] = acc.astype(o_ref.dtype)


def _mm_res_kernel(x_ref, w_ref, res_ref, gate_ref, o_ref, wb_ref, *, blocks_per_batch, n_batch):
    r = _mod_row(pl.program_id(1), blocks_per_batch, n_batch)
    _stage_weight(w_ref, wb_ref)
    o_ref[...] = res_ref[...] + gate_ref[pl.ds(r, 1), :] * _dot(x_ref[...], wb_ref[...])


def _mm_swiglu_kernel(x_ref, w1_ref, w3_ref, o_ref, w1b_ref, w3b_ref):
    _stage_weight(w1_ref, w1b_ref)
    _stage_weight(w3_ref, w3b_ref)
    x = x_ref[...]
    a = _dot(x, w1b_ref[...])
    o_ref[...] = (a * jax.nn.sigmoid(a) * _dot(x, w3b_ref[...])).astype(o_ref.dtype)


def _mm_tiles(rows, seq_rows, n):
    bm = _tile(math.gcd(rows, seq_rows), 512, BF16_SUBLANES)
    bn = _tile(n, 512, LANES)
    return bm, bn


def mm_plain(x, w, layer, col_off, n, rows, seq_rows):
    k = x.shape[1]
    bm, bn = _mm_tiles(rows, seq_rows, math.gcd(n, col_off) if col_off else n)
    return pl.pallas_call(
        _mm_plain_kernel, out_shape=jax.ShapeDtypeStruct((rows, n), BF16),
        grid=(n // bn, rows // bm),
        in_specs=[pl.BlockSpec((bm, k), lambda j, i: (i, 0)), _wspec(layer, k, bn, col_off // bn)],
        out_specs=pl.BlockSpec((bm, bn), lambda j, i: (i, j)),
        scratch_shapes=[pltpu.VMEM((k, bn), BF16)],
        compiler_params=_params(("parallel", "arbitrary")), name="mm_plain")(x, w)


def mm_qkv_rope(x, w, layer, col_off, rows, lat_rows, seq, tables):
    k = x.shape[1]
    n = 2 * ATTN_QK_DIM + ATTN_DIM
    bm = _tile(math.gcd(rows, seq), 512, BF16_SUBLANES)
    bn = _tile(math.gcd(ATTN_QK_DIM, col_off), 512, LANES)
    seq_blocks = seq // bm
    tab_spec = pl.BlockSpec((bm, LANES), lambda j, i: (i % seq_blocks, 0))
    kern = functools.partial(_mm_rope_kernel, rope_cols=2 * ATTN_QK_DIM // bn,
                             q_cols=ATTN_QK_DIM // bn, lat_rows=lat_rows // bm,
                             scale=ATTN_HEAD_DIM ** -0.5 * math.log2(math.e))
    return pl.pallas_call(
        kern, out_shape=jax.ShapeDtypeStruct((rows, n), BF16), grid=(n // bn, rows // bm),
        in_specs=[pl.BlockSpec((bm, k), lambda j, i: (i, 0)), _wspec(layer, k, bn, col_off // bn),
                  tab_spec, tab_spec, tab_spec],
        out_specs=pl.BlockSpec((bm, bn), lambda j, i: (i, j)),
        scratch_shapes=[pltpu.VMEM((k, bn), BF16)],
        compiler_params=_params(("parallel", "arbitrary")), name="mm_qkv_rope")(x, w, *tables)


def mm_residual(x, w, layer, res, mod_l, which, rows, seq_rows, n_batch):
    _, k, n = w.shape
    bm, bn = _mm_tiles(rows, seq_rows, n)
    gate_blk = which * (n // bn)
    kern = functools.partial(_mm_res_kernel, blocks_per_batch=seq_rows // bm, n_batch=n_batch)
    return pl.pallas_call(
        kern, out_shape=jax.ShapeDtypeStruct((rows, n), F32), grid=(n // bn, rows // bm),
        in_specs=[pl.BlockSpec((bm, k), lambda j, i: (i, 0)), _wspec(layer, k, bn),
                  pl.BlockSpec((bm, bn), lambda j, i: (i, j)),
                  pl.BlockSpec((COND_ROWS, bn), lambda j, i: (0, gate_blk + j))],
        out_specs=pl.BlockSpec((bm, bn), lambda j, i: (i, j)),
        scratch_shapes=[pltpu.VMEM((k, bn), BF16)],
        compiler_params=_params(("parallel", "arbitrary")), name="mm_residual")(x, w, res, mod_l)


def mm_swiglu(x, w1, w3, layer, rows, seq_rows):
    _, k, n = w1.shape
    bm, bn = _mm_tiles(rows, seq_rows, n)
    return pl.pallas_call(
        _mm_swiglu_kernel, out_shape=jax.ShapeDtypeStruct((rows, n), BF16),
        grid=(n // bn, rows // bm),
        in_specs=[pl.BlockSpec((bm, k), lambda j, i: (i, 0)), _wspec(layer, k, bn),
                  _wspec(layer, k, bn)],
        out_specs=pl.BlockSpec((bm, bn), lambda j, i: (i, j)),
        scratch_shapes=[pltpu.VMEM((k, bn), BF16), pltpu.VMEM((k, bn), BF16)],
        compiler_params=_params(("parallel", "arbitrary")), name="mm_swiglu")(x, w1, w3)


MOE_TILE = 512
MOE_GATHER_ROWS = 256
MOE_COMBINE_ROWS = 128


def moe_plan(route, n_experts):
    n_tok = route.shape[0]
    n_pairs = TOP_K * n_tok
    e_pair = route[:, 2:2 + TOP_K].astype(jnp.int32).reshape(n_pairs)
    onehot = (e_pair[:, None] == jnp.arange(n_experts, dtype=jnp.int32)[None, :])
    csum = jnp.cumsum(onehot.astype(jnp.int32), axis=0)
    rank = jnp.take_along_axis(csum, e_pair[:, None], axis=1)[:, 0] - 1
    counts = csum[-1]
    tiles_per = (counts + MOE_TILE - 1) // MOE_TILE
    tile_end = jnp.cumsum(tiles_per)
    pos = (tile_end - tiles_per)[e_pair] * MOE_TILE + rank
    n_rows = n_pairs + n_experts * MOE_TILE
    n_tiles = n_rows // MOE_TILE
    row_token = jnp.zeros((n_rows,), jnp.int32).at[pos].set(
        jnp.arange(n_pairs, dtype=jnp.int32) // TOP_K)
    tile_expert = jnp.minimum(
        jnp.searchsorted(tile_end, jnp.arange(n_tiles, dtype=jnp.int32), side="right"),
        n_experts - 1).astype(jnp.int32)
    return row_token, pos.astype(jnp.int32), tile_expert, tile_end[-1:].astype(jnp.int32)


def _row_copy(src_hbm, row, dst, sem):
    return pltpu.make_async_copy(src_hbm.at[pl.ds(row, 1)], dst, sem)


def _moe_gather_kernel(tok_ref, h_hbm, o_ref, buf, sem):
    rows = buf.shape[0]
    base = pl.program_id(0) * rows

    def issue(r, carry):
        _row_copy(h_hbm, tok_ref[base + r], buf.at[pl.ds(r, 1)], sem).start()
        return carry

    def wait(r, carry):
        _row_copy(h_hbm, 0, buf.at[pl.ds(r, 1)], sem).wait()
        return carry

    lax.fori_loop(0, rows, issue, 0)
    lax.fori_loop(0, rows, wait, 0)
    u = buf[...]
    half = u.shape[1]
    o_ref[:, :half] = lax.bitcast_convert_type(u & jnp.uint32(0xFFFF0000), F32).astype(BF16)
    o_ref[:, half:] = lax.bitcast_convert_type(u << 16, F32).astype(BF16)


def moe_gather(h_packed, row_token):
    n_rows = row_token.shape[0]
    half = h_packed.shape[1]
    rows = MOE_GATHER_ROWS
    return pl.pallas_call(
        _moe_gather_kernel, out_shape=jax.ShapeDtypeStruct((n_rows, 2 * half), BF16),
        grid_spec=pltpu.PrefetchScalarGridSpec(
            num_scalar_prefetch=1, grid=(n_rows // rows,),
            in_specs=[pl.BlockSpec(memory_space=pl.ANY)],
            out_specs=pl.BlockSpec((rows, 2 * half), lambda i, tok: (i, 0)),
            scratch_shapes=[pltpu.VMEM((rows, half), jnp.uint32), pltpu.SemaphoreType.DMA]),
        compiler_params=_params(("arbitrary",)), name="moe_gather")(row_token, h_packed)


def _moe_stage_weight(te_ref, w_ref, wb_ref):
    t = pl.program_id(1)
    prev = te_ref[jnp.maximum(t - 1, 0)]

    @pl.when(jnp.logical_or(t == 0, te_ref[t] != prev))
    def _():
        wb_ref[...] = w_ref[0, 0].astype(BF16)


def _moe_swiglu_kernel(te_ref, nu_ref, x_ref, w1_ref, w3_ref, o_ref, w1b_ref, w3b_ref):
    _moe_stage_weight(te_ref, w1_ref, w1b_ref)
    _moe_stage_weight(te_ref, w3_ref, w3b_ref)
    used = pl.program_id(1) < nu_ref[0]

    @pl.when(used)
    def _():
        x = x_ref[...]
        a = _dot(x, w1b_ref[...])
        o_ref[...] = (a * jax.nn.sigmoid(a) * _dot(x, w3b_ref[...])).astype(o_ref.dtype)

    @pl.when(jnp.logical_not(used))
    def _():
        o_ref[...] = jnp.zeros(o_ref.shape, o_ref.dtype)


def _moe_down_kernel(te_ref, nu_ref, x_ref, w_ref, o_ref, wb_ref):
    _moe_stage_weight(te_ref, w_ref, wb_ref)
    used = pl.program_id(1) < nu_ref[0]

    @pl.when(used)
    def _():
        o_ref[...] = _dot(x_ref[...], wb_ref[...])

    @pl.when(jnp.logical_not(used))
    def _():
        o_ref[...] = jnp.zeros(o_ref.shape, o_ref.dtype)


def _moe_wspec(layer, k, bn):
    return pl.BlockSpec((1, 1, k, bn), lambda j, t, te, nu: (layer, te[t], 0, j))


def moe_grouped_swiglu(xs, w1e, w3e, layer, tile_expert, n_used):
    n_rows, k = xs.shape
    n = w1e.shape[3]
    bn = _tile(n, 512, LANES)
    return pl.pallas_call(
        _moe_swiglu_kernel, out_shape=jax.ShapeDtypeStruct((n_rows, n), BF16),
        grid_spec=pltpu.PrefetchScalarGridSpec(
            num_scalar_prefetch=2, grid=(n // bn, n_rows // MOE_TILE),
            in_specs=[pl.BlockSpec((MOE_TILE, k), lambda j, t, te, nu: (t, 0)),
                      _moe_wspec(layer, k, bn), _moe_wspec(layer, k, bn)],
            out_specs=pl.BlockSpec((MOE_TILE, bn), lambda j, t, te, nu: (t, j)),
            scratch_shapes=[pltpu.VMEM((k, bn), BF16), pltpu.VMEM((k, bn), BF16)]),
        compiler_params=_params(("parallel", "arbitrary")), name="moe_swiglu",
    )(tile_expert, n_used, xs, w1e, w3e)


def moe_grouped_down(hm, w2e, layer, tile_expert, n_used):
    n_rows, k = hm.shape
    n = w2e.shape[3]
    bn = _tile(n, 512, LANES)
    return pl.pallas_call(
        _moe_down_kernel, out_shape=jax.ShapeDtypeStruct((n_rows, n), F32),
        grid_spec=pltpu.PrefetchScalarGridSpec(
            num_scalar_prefetch=2, grid=(n // bn, n_rows // MOE_TILE),
            in_specs=[pl.BlockSpec((MOE_TILE, k), lambda j, t, te, nu: (t, 0)),
                      _moe_wspec(layer, k, bn)],
            out_specs=pl.BlockSpec((MOE_TILE, bn), lambda j, t, te, nu: (t, j)),
            scratch_shapes=[pltpu.VMEM((k, bn), BF16)]),
        compiler_params=_params(("parallel", "arbitrary")), name="moe_down",
    )(tile_expert, n_used, hm, w2e)


def _moe_combine_kernel(pos_ref, y_hbm, res_ref, route_ref, gate_ref, o_ref, buf, sem, *,
                        blocks_per_batch, n_batch):
    i = pl.program_id(0)
    rows = res_ref.shape[0]
    base = i * rows * TOP_K

    def issue(r, carry):
        for s in range(TOP_K):
            _row_copy(y_hbm, pos_ref[base + TOP_K * r + s], buf.at[s, pl.ds(r, 1)], sem).start()
        return carry

    def wait(r, carry):
        for s in range(TOP_K):
            _row_copy(y_hbm, 0, buf.at[s, pl.ds(r, 1)], sem).wait()
        return carry

    lax.fori_loop(0, rows, issue, 0)
    lax.fori_loop(0, rows, wait, 0)
    w = route_ref[...]
    y = w[:, 0:1] * buf[0] + w[:, 1:2] * buf[1]
    r = _mod_row(i, blocks_per_batch, n_batch)
    o_ref[...] = res_ref[...] + gate_ref[pl.ds(r, 1), :] * y


def moe_combine(ys, pos, res, route, mod_l, which, rows, seq_rows, n_batch):
    d = ys.shape[1]
    bm = _tile(math.gcd(rows, seq_rows), MOE_COMBINE_ROWS, 8)
    kern = functools.partial(_moe_combine_kernel, blocks_per_batch=seq_rows // bm,
                             n_batch=n_batch)
    return pl.pallas_call(
        kern, out_shape=jax.ShapeDtypeStruct((rows, d), F32),
        grid_spec=pltpu.PrefetchScalarGridSpec(
            num_scalar_prefetch=1, grid=(rows // bm,),
            in_specs=[pl.BlockSpec(memory_space=pl.ANY),
                      pl.BlockSpec((bm, d), lambda i, p: (i, 0)),
                      pl.BlockSpec((bm, LANES), lambda i, p: (i, 0)),
                      pl.BlockSpec((COND_ROWS, d), lambda i, p: (0, which))],
            out_specs=pl.BlockSpec((bm, d), lambda i, p: (i, 0)),
            scratch_shapes=[pltpu.VMEM((TOP_K, bm, d), F32), pltpu.SemaphoreType.DMA]),
        compiler_params=_params(("arbitrary",)), name="moe_combine",
    )(pos, ys, res, route, mod_l)


def _merge_kernel(h_ref, f_ref, c_ref, a_ref, wg0, wg1, wg2, bg0, bg1, bg2, wf, wc, wa, o_ref,
                  wg0b, wg1b, wg2b, wfb, wcb, wab):
    for w_ref, wb_ref in ((wg0, wg0b), (wg1, wg1b), (wg2, wg2b), (wf, wfb), (wc, wcb), (wa, wab)):
        _stage_weight(w_ref, wb_ref)
    h = h_ref[...]

    def branch(wg, bg, x_ref, w):
        gate = jax.nn.sigmoid(_dot(h, wg[...]) + bg[0])
        return gate * _dot(x_ref[...], w[...])

    y = (branch(wg0b, bg0, f_ref, wfb) + branch(wg1b, bg1, c_ref, wcb)
         + branch(wg2b, bg2, a_ref, wab))
    o_ref[...] = y.astype(o_ref.dtype)


def merge_branches(h, four, conv, att, w_gate, b_gate, w_four_out, w_conv_out, w_attn_out, layer,
                   rows, seq_rows):
    d = h.shape[1]
    bm = _tile(math.gcd(rows, seq_rows), 512, BF16_SUBLANES)
    bn = _tile(d, 256, LANES)
    nb = d // bn
    depth = b_gate.shape[0]
    b3 = b_gate.reshape(depth, 1, 3 * d)

    def row_spec(width):
        return pl.BlockSpec((bm, width), lambda j, i: (i, 0))

    def col_spec(kdim, blk_off):
        return pl.BlockSpec((1, kdim, bn), lambda j, i: (layer, 0, blk_off + j),
                            pipeline_mode=pl.Buffered(1))

    kdims = (d, d, d, four.shape[1], conv.shape[1], att.shape[1])
    in_specs = ([row_spec(d), row_spec(four.shape[1]), row_spec(conv.shape[1]),
                 row_spec(att.shape[1])]
                + [col_spec(d, r * nb) for r in range(3)]
                + [col_spec(1, r * nb) for r in range(3)]
                + [col_spec(kd, 0) for kd in kdims[3:]])
    return pl.pallas_call(
        _merge_kernel, out_shape=jax.ShapeDtypeStruct((rows, d), BF16), grid=(nb, rows // bm),
        in_specs=in_specs, out_specs=pl.BlockSpec((bm, bn), lambda j, i: (i, j)),
        scratch_shapes=[pltpu.VMEM((kd, bn), BF16) for kd in kdims],
        compiler_params=_params(("parallel", "arbitrary")), name="merge_branches",
    )(h, four, conv, att, w_gate, w_gate, w_gate, b3, b3, b3, w_four_out, w_conv_out, w_attn_out)


def _fourier_tables(n_pos, group_dim, n2):
    n1 = n_pos // n2
    a = np.arange(n1)
    ang1 = 2.0 * np.pi * np.outer(a, a) / n1
    w1 = np.concatenate([np.cos(ang1), -np.sin(ang1)], axis=0)
    k = (np.arange(n1)[:, None] + n1 * np.arange(n2)[None, :])[:, :, None]
    ang2 = 2.0 * np.pi * ((k * np.arange(n2)[None, None, :]) % n_pos) / n_pos
    gc, gs = np.cos(ang2), np.sin(ang2)
    g = np.concatenate([np.concatenate([gc, gs], axis=2),
                        np.concatenate([-gs, gc], axis=2)], axis=1)
    c = np.arange(group_dim)
    angc = 2.0 * np.pi * np.outer(c, c) / group_dim
    scale = 1.0 / math.sqrt(n_pos * group_dim)
    as_bf16 = lambda t: jnp.asarray(t, dtype=F32).astype(BF16)
    return as_bf16(w1), as_bf16(g), as_bf16(np.cos(angc)), as_bf16(np.sin(angc)), scale


def _four1_kernel(w_ref, x_ref, o_ref):
    o_ref[0] = _dot(w_ref[...], x_ref[...]).astype(o_ref.dtype)


def _four2_kernel(*refs, scale, group_dim, complex_in):
    if complex_in:
        g_ref, zr_ref, zi_ref, cc_ref, sc_ref, o_ref = refs
        z = jnp.concatenate([zr_ref[0], zi_ref[0]], axis=0)
        g = g_ref[0]
    else:
        g_ref, zr_ref, cc_ref, sc_ref, o_ref = refs
        z = zr_ref[...]
        g = g_ref[0][:, :z.shape[0]]
    p = _dot(g, z)
    half = p.shape[0] // 2
    pr, pi = p[:half].astype(BF16), p[half:].astype(BF16)
    cc, sc = cc_ref[...], sc_ref[...]
    for grp in range(o_ref.shape[-1] // group_dim):
        sl = slice(grp * group_dim, (grp + 1) * group_dim)
        o = _dot(pr[:, sl], cc) + _dot(pi[:, sl], sc)
        o_ref[:, sl] = (o * scale).astype(o_ref.dtype)


def fourier_latent(f, n_batch, seq):
    total_rows = n_batch * seq
    fdim = f.shape[1]
    gd = fdim // FOURIER_GROUPS
    n2 = FFT_INNER
    n1 = seq // n2
    w1, g, cc, sc, scale = _fourier_tables(seq, gd, n2)
    wide = n2 * fdim
    bn = _tile(wide, 8192, LANES)
    z = pl.pallas_call(
        _four1_kernel, out_shape=jax.ShapeDtypeStruct((n_batch, 2 * n1, wide), BF16),
        grid=(n_batch, wide // bn),
        in_specs=[pl.BlockSpec((2 * n1, n1), lambda b, j: (0, 0)),
                  pl.BlockSpec((n1, bn), lambda b, j: (b, j))],
        out_specs=pl.BlockSpec((1, 2 * n1, bn), lambda b, j: (b, 0, j)),
        compiler_params=_params(("parallel", "parallel")), name="fourier_stage1",
    )(w1, f.reshape(f.shape[0] // n2, wide))
    z = z.reshape(n_batch * 2 * n1, n2, fdim)
    kern = functools.partial(_four2_kernel, scale=scale, group_dim=gd, complex_in=True)
    out = pl.pallas_call(
        kern, out_shape=jax.ShapeDtypeStruct((total_rows // n1, n1 * fdim), BF16),
        grid=(n_batch, n1),
        in_specs=[pl.BlockSpec((1, 2 * n2, 2 * n2), lambda b, k: (k, 0, 0)),
                  pl.BlockSpec((1, n2, fdim), lambda b, k: (b * 2 * n1 + k, 0, 0)),
                  pl.BlockSpec((1, n2, fdim), lambda b, k: (b * 2 * n1 + n1 + k, 0, 0)),
                  pl.BlockSpec((gd, gd), lambda b, k: (0, 0)),
                  pl.BlockSpec((gd, gd), lambda b, k: (0, 0))],
        out_specs=pl.BlockSpec((n2, fdim), lambda b, k: (b, k)),
        compiler_params=_params(("parallel", "parallel")), name="fourier_stage2",
    )(g, z, z, cc, sc)
    return out.reshape(total_rows, fdim)


def fourier_context(f, n_batch, ctx_len, lat_rows):
    fdim = f.shape[1]
    gd = fdim // FOURIER_GROUPS
    n2 = ctx_len
    _, g, cc, sc, scale = _fourier_tables(ctx_len, gd, n2)
    first = lat_rows // ctx_len
    kern = functools.partial(_four2_kernel, scale=scale, group_dim=gd, complex_in=False)
    return pl.pallas_call(
        kern, out_shape=jax.ShapeDtypeStruct((n_batch * ctx_len, fdim), BF16), grid=(n_batch,),
        in_specs=[pl.BlockSpec((1, 2 * n2, 2 * n2), lambda b: (0, 0, 0)),
                  pl.BlockSpec((ctx_len, fdim), lambda b: (first + b, 0)),
                  pl.BlockSpec((gd, gd), lambda b: (0, 0)),
                  pl.BlockSpec((gd, gd), lambda b: (0, 0))],
        out_specs=pl.BlockSpec((ctx_len, fdim), lambda b: (b, 0)),
        compiler_params=_params(("parallel",)), name="fourier_context",
    )(g, f, cc, sc)


def _conv_kernel(prev_ref, cur_ref, next_ref, w_ref, b_ref, g_ref, bb_ref, o_ref, z_ref, *,
                 starts, ends, chunk):
    i = pl.program_id(0)
    ts, c = o_ref.shape

    def glu(ref):
        v = ref[...].astype(F32)
        return v[:, :c] * jax.nn.sigmoid(v[:, c:])

    def any_eq(vals):
        hit = i == vals[0]
        for v in vals[1:]:
            hit = jnp.logical_or(hit, i == v)
        return hit

    z_ref[0:HALO_ROWS, :] = jnp.where(any_eq(starts), 0.0, glu(prev_ref))
    z_ref[HALO_ROWS:HALO_ROWS + ts, :] = glu(cur_ref)
    z_ref[HALO_ROWS + ts:, :] = jnp.where(any_eq(ends), 0.0, glu(next_ref))
    bias, gamma, beta = b_ref[...], g_ref[...], bb_ref[...]
    for r0 in range(0, ts, chunk):
        acc = jnp.zeros((chunk, c), F32)
        for t in range(CONV_WIDTH):
            lo = r0 + t + HALO_ROWS - CONV_PAD
            acc = acc + w_ref[t:t + 1, :] * z_ref[lo:lo + chunk, :]
        acc = acc + bias
        mu = jnp.mean(acc, axis=-1, keepdims=True)
        dev = acc - mu
        var = jnp.mean(dev * dev, axis=-1, keepdims=True)
        y = dev * lax.rsqrt(var + LN_EPS) * gamma + beta
        o_ref[r0:r0 + chunk, :] = (y * jax.nn.sigmoid(y)).astype(o_ref.dtype)


def conformer_conv(cab, w_dw, b_dw, ln_g, ln_b, rows, seq_lens):
    c = cab.shape[1] // 2
    ts = _tile(math.gcd(*seq_lens) if len(seq_lens) > 1 else seq_lens[0], 256, HALO_ROWS)
    per = ts // HALO_ROWS
    starts, ends, pos = [], [], 0
    for n in seq_lens:
        starts.append(pos // ts)
        pos += n
        ends.append(pos // ts - 1)
    assert pos == rows
    last_halo = rows // HALO_ROWS - 1
    kern = functools.partial(_conv_kernel, starts=tuple(starts), ends=tuple(ends), chunk=16)
    vec = lambda v: v.reshape(1, c)
    vspec = pl.BlockSpec((1, c), lambda i: (0, 0))
    return pl.pallas_call(
        kern, out_shape=jax.ShapeDtypeStruct((rows, c), BF16), grid=(rows // ts,),
        in_specs=[pl.BlockSpec((HALO_ROWS, 2 * c), lambda i: (jnp.maximum(i * per - 1, 0), 0)),
                  pl.BlockSpec((ts, 2 * c), lambda i: (i, 0)),
                  pl.BlockSpec((HALO_ROWS, 2 * c),
                               lambda i: (jnp.minimum((i + 1) * per, last_halo), 0)),
                  pl.BlockSpec((CONV_WIDTH, c), lambda i: (0, 0)), vspec, vspec, vspec],
        out_specs=pl.BlockSpec((ts, c), lambda i: (i, 0)),
        scratch_shapes=[pltpu.VMEM((ts + 2 * HALO_ROWS, c), F32)],
        compiler_params=_params(("parallel",)), name="conformer_conv",
    )(cab, cab, cab, w_dw, vec(b_dw), vec(ln_g), vec(ln_b))


def _lane_tile(x, reps):
    return x if reps == 1 else jnp.concatenate([x] * reps, axis=1)


def _attn_kernel(*refs, lam_init, tk, has_latent):
    if has_latent:
        (q_ref, k_ref, v_ref, kc_ref, vc_ref, lam_ref, g_ref, o_ref,
         m_ref, l_ref, acc_ref, s_ref) = refs
    else:
        q_ref, kc_ref, vc_ref, lam_ref, g_ref, o_ref, m_ref, l_ref, acc_ref, s_ref = refs
    hd = ATTN_HEAD_DIM
    m_ref[...] = jnp.full(m_ref.shape, -jnp.inf, F32)
    l_ref[...] = jnp.zeros(l_ref.shape, F32)
    acc_ref[...] = jnp.zeros(acc_ref.shape, F32)

    def scores(slot, k):
        n = k.shape[0]
        for mp in range(2):
            s_ref[slot, mp, :, :n] = lax.dot_general(
                q_ref[:, mp * hd:(mp + 1) * hd], k[:, mp * hd:(mp + 1) * hd],
                (((1,), (1,)), ((), ())), preferred_element_type=F32)

    def softmax_pv(slot, v):
        n = v.shape[0]
        for mp in range(2):
            s = s_ref[slot, mp, :, :n]
            m_prev = m_ref[mp]
            m_new = jnp.maximum(m_prev, jnp.max(s, axis=-1, keepdims=True))
            alpha = jnp.exp2(m_prev - m_new)
            p = jnp.exp2(s - _lane_tile(m_new, n // LANES))
            l_ref[mp] = alpha * l_ref[mp] + jnp.sum(p, axis=-1, keepdims=True)
            acc_ref[mp] = (acc_ref[mp] * _lane_tile(alpha, ATTN_V_DIM // LANES)
                           + _dot(p.astype(BF16), v))
            m_ref[mp] = m_new

    if has_latent:
        n_chunks = k_ref.shape[0] // tk
        assert n_chunks >= 2 and n_chunks % 2 == 0

        def chunk(ref, c):
            off = c * tk if isinstance(c, int) else pl.multiple_of(c * tk, tk)
            return ref[pl.ds(off, tk), :]

        scores(0, chunk(k_ref, 0))

        def body(i, carry):
            c = 2 * i
            scores(1, chunk(k_ref, c + 1))
            softmax_pv(0, chunk(v_ref, c))
            scores(0, chunk(k_ref, c + 2))
            softmax_pv(1, chunk(v_ref, c + 1))
            return carry

        lax.fori_loop(0, n_chunks // 2 - 1, body, 0)
        c = n_chunks - 2
        scores(1, chunk(k_ref, c + 1))
        softmax_pv(0, chunk(v_ref, c))
        scores(0, kc_ref[...])
        softmax_pv(1, chunk(v_ref, c + 1))
        softmax_pv(0, vc_ref[...])
    else:
        scores(0, kc_ref[...])
        softmax_pv(0, vc_ref[...])

    lam_v = lam_ref[...]
    lam = (jnp.exp(jnp.sum(lam_v[0:1] * lam_v[1:2], axis=-1, keepdims=True))
           - jnp.exp(jnp.sum(lam_v[2:3] * lam_v[3:4], axis=-1, keepdims=True)) + lam_init)
    reps = ATTN_V_DIM // LANES
    o = (acc_ref[0] * _lane_tile(1.0 / l_ref[0], reps)
         - lam * (acc_ref[1] * _lane_tile(1.0 / l_ref[1], reps)))
    o = _rms(o, g_ref[...]) * (1.0 - lam_init)
    o_ref[...] = o.astype(o_ref.dtype)


def _attn_scratch(tq, tk, slots):
    return [pltpu.VMEM((2, tq, LANES), F32), pltpu.VMEM((2, tq, LANES), F32),
            pltpu.VMEM((2, tq, ATTN_V_DIM), F32), pltpu.VMEM((slots, 2, tq, tk), F32)]


def attention_latent(qkv, lam_vecs, subln_g, lam_init, n_batch, seq, ctx_len):
    total_rows = n_batch * seq
    tq = _tile(seq, 512, BF16_SUBLANES)
    tk = _tile(seq, 512, LANES)
    qb = seq // tq
    kblk = ATTN_QK_DIM // ATTN_V_DIM
    vblk = 2 * ATTN_QK_DIM // ATTN_V_DIM
    first_ctx = n_batch * seq // ctx_len
    kern = functools.partial(_attn_kernel, lam_init=lam_init, tk=tk, has_latent=True)
    w = ATTN_V_DIM
    return pl.pallas_call(
        kern, out_shape=jax.ShapeDtypeStruct((total_rows, ATTN_DIM), BF16),
        grid=(n_batch, ATTN_HEADS, qb),
        in_specs=[pl.BlockSpec((tq, w), lambda b, h, i: (b * qb + i, h)),
                  pl.BlockSpec((seq, w), lambda b, h, i: (b, kblk + h)),
                  pl.BlockSpec((seq, w), lambda b, h, i: (b, vblk + h)),
                  pl.BlockSpec((ctx_len, w), lambda b, h, i: (first_ctx + b, kblk + h)),
                  pl.BlockSpec((ctx_len, w), lambda b, h, i: (first_ctx + b, vblk + h)),
                  pl.BlockSpec((4, ATTN_HEAD_DIM), lambda b, h, i: (0, 0)),
                  pl.BlockSpec((1, w), lambda b, h, i: (0, 0))],
        out_specs=pl.BlockSpec((tq, w), lambda b, h, i: (b * qb + i, h)),
        scratch_shapes=_attn_scratch(tq, max(tk, ctx_len), 2),
        compiler_params=_params(("parallel", "parallel", "parallel")), name="attention_latent",
    )(qkv, qkv, qkv, qkv, qkv, lam_vecs, subln_g.reshape(1, w))


def attention_context(qkv, lam_vecs, subln_g, lam_init, n_batch, ctx_len, lat_rows):
    kblk = ATTN_QK_DIM // ATTN_V_DIM
    vblk = 2 * ATTN_QK_DIM // ATTN_V_DIM
    first = lat_rows // ctx_len
    kern = functools.partial(_attn_kernel, lam_init=lam_init, tk=ctx_len, has_latent=False)
    w = ATTN_V_DIM
    return pl.pallas_call(
        kern, out_shape=jax.ShapeDtypeStruct((n_batch * ctx_len, ATTN_DIM), BF16),
        grid=(n_batch, ATTN_HEADS),
        in_specs=[pl.BlockSpec((ctx_len, w), lambda b, h: (first + b, h)),
                  pl.BlockSpec((ctx_len, w), lambda b, h: (first + b, kblk + h)),
                  pl.BlockSpec((ctx_len, w), lambda b, h: (first + b, vblk + h)),
                  pl.BlockSpec((4, ATTN_HEAD_DIM), lambda b, h: (0, 0)),
                  pl.BlockSpec((1, w), lambda b, h: (0, 0))],
        out_specs=pl.BlockSpec((ctx_len, w), lambda b, h: (b, h)),
        scratch_shapes=_attn_scratch(ctx_len, ctx_len, 1),
        compiler_params=_params(("parallel", "parallel")), name="attention_context",
    )(qkv, qkv, qkv, lam_vecs, subln_g.reshape(1, w))


def _rope_tables(n):
    pos = np.arange(n)
    inv = ROPE_BASE ** (-np.arange(ROPE_FREQS, dtype=np.float64) / ROPE_FREQS)
    ang = [(pos // GRID_W)[:, None] * inv, (pos % GRID_W)[:, None] * inv]
    zero = np.zeros((n, ROPE_FREQS))
    cos = np.concatenate([np.cos(ang[0])] * 2 + [np.cos(ang[1])] * 2, axis=1)
    s_up = np.concatenate([zero, np.sin(ang[0]), zero, np.sin(ang[1])], axis=1)
    s_dn = np.concatenate([-np.sin(ang[0]), zero, -np.sin(ang[1]), zero], axis=1)
    return tuple(jnp.asarray(t, dtype=F32) for t in (cos, s_up, s_dn))


def kernel(x, c, ctx, c_ctx, norm1_g, w_mod, b_mod, w_in, w_gate, b_gate, w_four_out, w_dw, b_dw,
           conv_ln_g, conv_ln_b, w_conv_out, lam_q1, lam_k1, lam_q2, lam_k2, subln_g, w_attn_out,
           w_o, norm2_g, w1, w3, w2, w_router, b_router, w1e, w3e, w2e, norm_f_g):
    n_batch, seq, d = x.shape
    ctx_len = ctx.shape[1]
    depth = w_in.shape[0]
    lat_rows, ctx_rows = n_batch * seq, n_batch * ctx_len
    all_rows = lat_rows + ctx_rows
    fdim = w_four_out.shape[1]
    cdim = w_conv_out.shape[1]
    assert n_batch < COND_ROWS and ctx_len % HALO_ROWS == 0 and seq % FFT_INNER == 0

    xa = jnp.concatenate([x.reshape(lat_rows, d), ctx.reshape(ctx_rows, d)], axis=0)
    cond = jnp.zeros((COND_ROWS, d), F32).at[:n_batch].set(c).at[n_batch].set(c_ctx)
    mod = ada_params_all(cond, w_mod, b_mod)
    rope = _rope_tables(seq)

    for l in range(depth):
        last = l == depth - 1
        rows = lat_rows if last else all_rows
        seqs = [seq] * n_batch + ([] if last else [ctx_len] * n_batch)
        lam_init = 0.8 - 0.6 * math.exp(-0.3 * l)
        lam_vecs = jnp.stack([lam_q1[l], lam_k1[l], lam_q2[l], lam_k2[l]]).astype(F32)

        h = norm_modulate(xa, norm1_g[l], mod[l], 0, all_rows, seq, n_batch)
        f = mm_plain(h, w_in, l, 0, fdim, rows, seq)
        cab = mm_plain(h, w_in, l, fdim, 2 * cdim, rows, seq)
        qkv = mm_qkv_rope(h, w_in, l, fdim + 2 * cdim, all_rows, lat_rows, seq, rope)
        four = fourier_latent(f, n_batch, seq)
        att = attention_latent(qkv, lam_vecs, subln_g[l], lam_init, n_batch, seq, ctx_len)
        if not last:
            four = jnp.concatenate([four, fourier_context(f, n_batch, ctx_len, lat_rows)], axis=0)
            att = jnp.concatenate([att, attention_context(qkv, lam_vecs, subln_g[l], lam_init,
                                                          n_batch, ctx_len, lat_rows)], axis=0)
        conv = conformer_conv(cab, w_dw[l], b_dw[l], conv_ln_g[l], conv_ln_b[l], rows, seqs)
        y = merge_branches(h, four, conv, att, w_gate, b_gate, w_four_out, w_conv_out,
                           w_attn_out, l, rows, seq)
        xa = mm_residual(y, w_o, l, xa, mod[l], 2, rows, seq, n_batch)

        i = l // 2
        if l % 2 == 0:
            h2 = norm_modulate(xa, norm2_g[l], mod[l], 3, rows, seq, n_batch)
            hm = mm_swiglu(h2, w1, w3, i, rows, seq)
            xa = mm_residual(hm, w2, i, xa, mod[l], 5, rows, seq, n_batch)
        else:
            h2p, route = norm_modulate(xa, norm2_g[l], mod[l], 3, rows, seq, n_batch,
                                       router=(w_router[i], b_router[i]))
            row_token, pos, tile_expert, n_used = moe_plan(route, w1e.shape[1])
            xs = moe_gather(h2p, row_token)
            hm = moe_grouped_swiglu(xs, w1e, w3e, i, tile_expert, n_used)
            ys = moe_grouped_down(hm, w2e, i, tile_expert, n_used)
            xa = moe_combine(ys, pos, xa, route, mod[l], 5, rows, seq, n_batch)

    return final_norm(xa, norm_f_g, lat_rows).reshape(n_batch, seq, d)
```

```python
import functools
import math

import numpy as np
import jax
import jax.numpy as jnp
from jax import lax
from jax.experimental import pallas as pl
from jax.experimental.pallas import tpu as pltpu

F32 = jnp.float32
BF16 = jnp.bfloat16

GRID_W = 64
EPS = 1e-6
LN_EPS = 1e-5
N_MOD = 6
FOURIER_GROUPS = 4
CONV_WIDTH = 31
CONV_PAD = CONV_WIDTH // 2
ATTN_HEADS = 8
ATTN_HEAD_DIM = 128
ATTN_V_DIM = 2 * ATTN_HEAD_DIM
ATTN_QK_DIM = ATTN_HEADS * 2 * ATTN_HEAD_DIM
ATTN_DIM = ATTN_HEADS * ATTN_V_DIM
ROPE_BASE = 10000.0
ROPE_FREQS = ATTN_HEAD_DIM // 4
N_EXPERTS = 8
TOP_K = 2

LANES = 128
BF16_SUBLANES = 16
V7X_VMEM_LIMIT_BYTES = 56 * 1024 * 1024
FFT_INNER = 128
COND_ROWS = 8
HALO_ROWS = BF16_SUBLANES


def _tile(n, pref, align):
    t = (min(pref, n) // align) * align
    while t > align and n % t:
        t -= align
    assert t >= align and n % t == 0, (n, pref, align)
    return t


def _params(semantics):
    return pltpu.CompilerParams(dimension_semantics=semantics,
                                vmem_limit_bytes=V7X_VMEM_LIMIT_BYTES)


def _dot(a, b):
    return jnp.dot(a, b, preferred_element_type=F32)


def _ada_kernel(c_ref, w_ref, b_ref, o_ref):
    c = c_ref[...]
    s = (c * jax.nn.sigmoid(c)).astype(BF16)
    o_ref[0] = _dot(s, w_ref[0].astype(BF16)) + b_ref[0]


def ada_params_all(cond, w_mod, b_mod):
    depth, d, n = w_mod.shape
    bn = _tile(n, 512, LANES)
    return pl.pallas_call(
        _ada_kernel,
        out_shape=jax.ShapeDtypeStruct((depth, COND_ROWS, n), F32),
        grid=(depth, n // bn),
        in_specs=[pl.BlockSpec((COND_ROWS, d), lambda l, j: (0, 0)),
                  pl.BlockSpec((1, d, bn), lambda l, j: (l, 0, j)),
                  pl.BlockSpec((1, 1, bn), lambda l, j: (l, 0, j))],
        out_specs=pl.BlockSpec((1, COND_ROWS, bn), lambda l, j: (l, 0, j)),
        compiler_params=_params(("parallel", "parallel")),
        name="ada_params",
    )(cond, w_mod, b_mod.reshape(depth, 1, n))


def _mod_row(i, blocks_per_batch, n_batch):
    return jnp.minimum(i // blocks_per_batch, n_batch)


def _rms(x, g):
    ms = jnp.mean(x * x, axis=-1, keepdims=True)
    return x * lax.rsqrt(ms + EPS) * g


def _normmod_kernel(x_ref, g_ref, sh_ref, sc_ref, o_ref, *, blocks_per_batch, n_batch):
    r = _mod_row(pl.program_id(0), blocks_per_batch, n_batch)
    y = _rms(x_ref[...], g_ref[...])
    h = y * (1.0 + sc_ref[pl.ds(r, 1), :]) + sh_ref[pl.ds(r, 1), :]
    o_ref[...] = h.astype(o_ref.dtype)


def _normmod_router_kernel(x_ref, g_ref, sh_ref, sc_ref, wr_ref, br_ref, o_ref, route_ref, *,
                           blocks_per_batch, n_batch):
    r = _mod_row(pl.program_id(0), blocks_per_batch, n_batch)
    y = _rms(x_ref[...], g_ref[...])
    h = y * (1.0 + sc_ref[pl.ds(r, 1), :]) + sh_ref[pl.ds(r, 1), :]
    bits = lax.bitcast_convert_type(h.astype(BF16).astype(F32), jnp.uint32)
    half = bits.shape[1] // 2
    o_ref[...] = bits[:, :half] | (bits[:, half:] >> 16)
    logits = jnp.dot(h, wr_ref[...], preferred_element_type=F32,
                     precision=lax.Precision.HIGHEST) + br_ref[...]
    lane = lax.broadcasted_iota(jnp.int32, logits.shape, 1)
    v1 = jnp.max(logits, axis=-1, keepdims=True)
    i1 = jnp.min(jnp.where(logits == v1, lane, LANES), axis=-1, keepdims=True)
    rest = jnp.where(lane == i1, -jnp.inf, logits)
    v2 = jnp.max(rest, axis=-1, keepdims=True)
    i2 = jnp.min(jnp.where(rest == v2, lane, LANES), axis=-1, keepdims=True)
    e = jnp.exp(v2 - v1)
    w1 = 1.0 / (1.0 + e)
    route_ref[...] = (jnp.where(lane == 0, w1, 0.0) + jnp.where(lane == 1, e * w1, 0.0)
                      + jnp.where(lane == 2, i1.astype(F32), 0.0)
                      + jnp.where(lane == 3, i2.astype(F32), 0.0))


def _final_norm_kernel(x_ref, g_ref, o_ref):
    o_ref[...] = _rms(x_ref[...], g_ref[...])


def norm_modulate(xa, g, mod_l, which, rows, seq_rows, n_batch, router=None):
    d = xa.shape[1]
    bm = _tile(math.gcd(rows, seq_rows), 256, BF16_SUBLANES)
    kw = dict(blocks_per_batch=seq_rows // bm, n_batch=n_batch)
    in_specs = [pl.BlockSpec((bm, d), lambda i: (i, 0)),
                pl.BlockSpec((1, d), lambda i: (0, 0)),
                pl.BlockSpec((COND_ROWS, d), lambda i: (0, which)),
                pl.BlockSpec((COND_ROWS, d), lambda i: (0, which + 1))]
    args = [xa, g.reshape(1, d), mod_l, mod_l]
    h_shape = jax.ShapeDtypeStruct((rows, d), BF16)
    h_spec = pl.BlockSpec((bm, d), lambda i: (i, 0))
    if router is None:
        return pl.pallas_call(
            functools.partial(_normmod_kernel, **kw), out_shape=h_shape, grid=(rows // bm,),
            in_specs=in_specs, out_specs=h_spec, compiler_params=_params(("parallel",)),
            name="norm_modulate")(*args)
    w_r, b_r = router
    ne = w_r.shape[1]
    w_pad = jnp.zeros((d, LANES), F32).at[:, :ne].set(w_r)
    b_pad = jnp.full((1, LANES), -jnp.inf, F32).at[0, :ne].set(b_r)
    return pl.pallas_call(
        functools.partial(_normmod_router_kernel, **kw),
        out_shape=(jax.ShapeDtypeStruct((rows, d // 2), jnp.uint32),
                   jax.ShapeDtypeStruct((rows, LANES), F32)), grid=(rows // bm,),
        in_specs=in_specs + [pl.BlockSpec((d, LANES), lambda i: (0, 0)),
                             pl.BlockSpec((1, LANES), lambda i: (0, 0))],
        out_specs=(pl.BlockSpec((bm, d // 2), lambda i: (i, 0)),
                   pl.BlockSpec((bm, LANES), lambda i: (i, 0))),
        compiler_params=_params(("parallel",)), name="norm_modulate_router")(*args, w_pad, b_pad)


def final_norm(xa, g, rows):
    d = xa.shape[1]
    bm = _tile(rows, 256, 8)
    return pl.pallas_call(
        _final_norm_kernel, out_shape=jax.ShapeDtypeStruct((rows, d), F32), grid=(rows // bm,),
        in_specs=[pl.BlockSpec((bm, d), lambda i: (i, 0)), pl.BlockSpec((1, d), lambda i: (0, 0))],
        out_specs=pl.BlockSpec((bm, d), lambda i: (i, 0)),
        compiler_params=_params(("parallel",)), name="final_norm")(xa, g.reshape(1, d))


def _stage_weight(w_ref, wb_ref):
    @pl.when(pl.program_id(1) == 0)
    def _():
        wb_ref[...] = w_ref[0].astype(BF16)


def _wspec(layer, k, bn, col_blk_off=0):
    return pl.BlockSpec((1, k, bn), lambda j, i: (layer, 0, col_blk_off + j))


def _mm_plain_kernel(x_ref, w_ref, o_ref, wb_ref):
    _stage_weight(w_ref, wb_ref)
    o_ref[...] = _dot(x_ref[...], wb_ref[...]).astype(o_ref.dtype)


def _mm_rope_kernel(x_ref, w_ref, c_ref, s1_ref, s2_ref, o_ref, wb_ref, *, rope_cols):
    j = pl.program_id(0)
    _stage_weight(w_ref, wb_ref)

    @pl.when(j < rope_cols)
    def _():
        acc = _dot(x_ref[...], wb_ref[...])
        c, s1, s2 = c_ref[...], s1_ref[...], s2_ref[...]
        for g in range(acc.shape[1] // LANES):
            t = acc[:, g * LANES:(g + 1) * LANES]
            r = (t * c + pltpu.roll(t, ROPE_FREQS, 1) * s1
                 + pltpu.roll(t, LANES - ROPE_FREQS, 1) * s2)
            o_ref[:, g * LANES:(g + 1) * LANES] = r.astype(o_ref.dtype)

    @pl.when(j >= rope_cols)
    def _():
        o_ref[...] = _dot(x_ref[...], wb_ref[...]).astype(o_ref.dtype)


def _mm_res_kernel(x_ref, w_ref, res_ref, gate_ref, o_ref, wb_ref, *, blocks_per_batch, n_batch):
    r = _mod_row(pl.program_id(1), blocks_per_batch, n_batch)
    _stage_weight(w_ref, wb_ref)
    o_ref[...] = res_ref[...] + gate_ref[pl.ds(r, 1), :] * _dot(x_ref[...], wb_ref[...])


def _mm_swiglu_kernel(x_ref, w1_ref, w3_ref, o_ref, w1b_ref, w3b_ref):
    _stage_weight(w1_ref, w1b_ref)
    _stage_weight(w3_ref, w3b_ref)
    x = x_ref[...]
    a = _dot(x, w1b_ref[...])
    o_ref[...] = (a * jax.nn.sigmoid(a) * _dot(x, w3b_ref[...])).astype(o_ref.dtype)


def _mm_tiles(rows, seq_rows, n):
    bm = _tile(math.gcd(rows, seq_rows), 512, BF16_SUBLANES)
    bn = _tile(n, 512, LANES)
    return bm, bn


def mm_plain(x, w, layer, col_off, n, rows, seq_rows):
    k = x.shape[1]
    bm, bn = _mm_tiles(rows, seq_rows, math.gcd(n, col_off) if col_off else n)
    return pl.pallas_call(
        _mm_plain_kernel, out_shape=jax.ShapeDtypeStruct((rows, n), BF16),
        grid=(n // bn, rows // bm),
        in_specs=[pl.BlockSpec((bm, k), lambda j, i: (i, 0)), _wspec(layer, k, bn, col_off // bn)],
        out_specs=pl.BlockSpec((bm, bn), lambda j, i: (i, j)),
        scratch_shapes=[pltpu.VMEM((k, bn), BF16)],
        compiler_params=_params(("parallel", "arbitrary")), name="mm_plain")(x, w)


def mm_qkv_rope(x, w, layer, col_off, rows, lat_rows, seq, tables):
    k = x.shape[1]
    n = 2 * ATTN_QK_DIM + ATTN_DIM
    bm = _tile(math.gcd(rows, seq), 512, BF16_SUBLANES)
    bn = _tile(math.gcd(ATTN_QK_DIM, col_off), 512, LANES)
    seq_blocks, lat_blocks, q_cols = seq // bm, lat_rows // bm, ATTN_QK_DIM // bn
    scale = ATTN_HEAD_DIM ** -0.5 * math.log2(math.e)
    cos, s_up, s_dn = tables
    zeros = jnp.zeros((bm, LANES), F32)
    cos_all = jnp.concatenate([cos * scale, cos, zeros + scale, zeros + 1.0], axis=0)
    sup_all = jnp.concatenate([s_up * scale, s_up, zeros, zeros], axis=0)
    sdn_all = jnp.concatenate([s_dn * scale, s_dn, zeros, zeros], axis=0)

    def tab_block(j, i):
        is_q = j < q_cols
        lat = jnp.where(is_q, 0, seq_blocks) + i % seq_blocks
        ctx = jnp.where(is_q, 2 * seq_blocks, 2 * seq_blocks + 1)
        return jnp.where(i < lat_blocks, lat, ctx), 0

    tab_spec = pl.BlockSpec((bm, LANES), tab_block)
    kern = functools.partial(_mm_rope_kernel, rope_cols=2 * q_cols)
    return pl.pallas_call(
        kern, out_shape=jax.ShapeDtypeStruct((rows, n), BF16), grid=(n // bn, rows // bm),
        in_specs=[pl.BlockSpec((bm, k), lambda j, i: (i, 0)), _wspec(layer, k, bn, col_off // bn),
                  tab_spec, tab_spec, tab_spec],
        out_specs=pl.BlockSpec((bm, bn), lambda j, i: (i, j)),
        scratch_shapes=[pltpu.VMEM((k, bn), BF16)],
        compiler_params=_params(("parallel", "arbitrary")), name="mm_qkv_rope",
    )(x, w, cos_all, sup_all, sdn_all)


def mm_residual(x, w, layer, res, mod_l, which, rows, seq_rows, n_batch):
    _, k, n = w.shape
    bm, bn = _mm_tiles(rows, seq_rows, n)
    gate_blk = which * (n // bn)
    kern = functools.partial(_mm_res_kernel, blocks_per_batch=seq_rows // bm, n_batch=n_batch)
    return pl.pallas_call(
        kern, out_shape=jax.ShapeDtypeStruct((rows, n), F32), grid=(n // bn, rows // bm),
        in_specs=[pl.BlockSpec((bm, k), lambda j, i: (i, 0)), _wspec(layer, k, bn),
                  pl.BlockSpec((bm, bn), lambda j, i: (i, j)),
                  pl.BlockSpec((COND_ROWS, bn), lambda j, i: (0, gate_blk + j))],
        out_specs=pl.BlockSpec((bm, bn), lambda j, i: (i, j)),
        scratch_shapes=[pltpu.VMEM((k, bn), BF16)],
        compiler_params=_params(("parallel", "arbitrary")), name="mm_residual")(x, w, res, mod_l)


def mm_swiglu(x, w1, w3, layer, rows, seq_rows):
    _, k, n = w1.shape
    bm, bn = _mm_tiles(rows, seq_rows, n)
    return pl.pallas_call(
        _mm_swiglu_kernel, out_shape=jax.ShapeDtypeStruct((rows, n), BF16),
        grid=(n // bn, rows // bm),
        in_specs=[pl.BlockSpec((bm, k), lambda j, i: (i, 0)), _wspec(layer, k, bn),
                  _wspec(layer, k, bn)],
        out_specs=pl.BlockSpec((bm, bn), lambda j, i: (i, j)),
        scratch_shapes=[pltpu.VMEM((k, bn), BF16), pltpu.VMEM((k, bn), BF16)],
        compiler_params=_params(("parallel", "arbitrary")), name="mm_swiglu")(x, w1, w3)


MOE_TILE = 512
MOE_GATHER_ROWS = 256
MOE_COMBINE_ROWS = 128
DMA_LOOP_UNROLL = 8


def moe_plan(route, n_experts):
    n_tok = route.shape[0]
    n_pairs = TOP_K * n_tok
    e_pair = route[:, 2:2 + TOP_K].astype(jnp.int32).reshape(n_pairs)
    onehot = (e_pair[:, None] == jnp.arange(n_experts, dtype=jnp.int32)[None, :])
    csum = jnp.cumsum(onehot.astype(jnp.int32), axis=0)
    rank = jnp.take_along_axis(csum, e_pair[:, None], axis=1)[:, 0] - 1
    counts = csum[-1]
    tiles_per = (counts + MOE_TILE - 1) // MOE_TILE
    tile_end = jnp.cumsum(tiles_per)
    pos = (tile_end - tiles_per)[e_pair] * MOE_TILE + rank
    n_rows = n_pairs + n_experts * MOE_TILE
    n_tiles = n_rows // MOE_TILE
    row_token = jnp.zeros((n_rows,), jnp.int32).at[pos].set(
        jnp.arange(n_pairs, dtype=jnp.int32) // TOP_K)
    tile_expert = jnp.minimum(
        jnp.searchsorted(tile_end, jnp.arange(n_tiles, dtype=jnp.int32), side="right"),
        n_experts - 1).astype(jnp.int32)
    return row_token, pos.astype(jnp.int32), tile_expert, tile_end[-1:].astype(jnp.int32)


def _row_copy(src_hbm, row, dst, sem):
    return pltpu.make_async_copy(src_hbm.at[pl.ds(row, 1)], dst, sem)


def _moe_gather_kernel(tok_ref, h_hbm, o_ref, buf, sems):
    i = pl.program_id(0)
    n_blocks = pl.num_programs(0)
    rows = buf.shape[1]

    def issue_block(blk, slot):
        base = blk * rows

        def issue(r, carry):
            _row_copy(h_hbm, tok_ref[base + r], buf.at[slot, pl.ds(r, 1)], sems.at[slot]).start()
            return carry

        lax.fori_loop(0, rows, issue, 0, unroll=DMA_LOOP_UNROLL)

    @pl.when(i == 0)
    def _():
        issue_block(0, 0)

    @pl.when(i + 1 < n_blocks)
    def _():
        issue_block(i + 1, (i + 1) % 2)

    slot = i % 2

    def wait(r, carry):
        _row_copy(h_hbm, 0, buf.at[slot, pl.ds(r, 1)], sems.at[slot]).wait()
        return carry

    lax.fori_loop(0, rows, wait, 0, unroll=DMA_LOOP_UNROLL)
    u = buf[slot]
    half = u.shape[1]
    o_ref[:, :half] = lax.bitcast_convert_type(u & jnp.uint32(0xFFFF0000), F32).astype(BF16)
    o_ref[:, half:] = lax.bitcast_convert_type(u << 16, F32).astype(BF16)


def moe_gather(h_packed, row_token):
    n_rows = row_token.shape[0]
    half = h_packed.shape[1]
    rows = MOE_GATHER_ROWS
    return pl.pallas_call(
        _moe_gather_kernel, out_shape=jax.ShapeDtypeStruct((n_rows, 2 * half), BF16),
        grid_spec=pltpu.PrefetchScalarGridSpec(
            num_scalar_prefetch=1, grid=(n_rows // rows,),
            in_specs=[pl.BlockSpec(memory_space=pl.ANY)],
            out_specs=pl.BlockSpec((rows, 2 * half), lambda i, tok: (i, 0)),
            scratch_shapes=[pltpu.VMEM((2, rows, half), jnp.uint32),
                            pltpu.SemaphoreType.DMA((2,))]),
        compiler_params=_params(("arbitrary",)), name="moe_gather")(row_token, h_packed)


def _moe_stage_weight(te_ref, w_ref, wb_ref):
    t = pl.program_id(1)
    prev = te_ref[jnp.maximum(t - 1, 0)]

    @pl.when(jnp.logical_or(t == 0, te_ref[t] != prev))
    def _():
        wb_ref[...] = w_ref[0, 0].astype(BF16)


def _moe_swiglu_kernel(te_ref, nu_ref, x_ref, w1_ref, w3_ref, o_ref, w1b_ref, w3b_ref):
    _moe_stage_weight(te_ref, w1_ref, w1b_ref)
    _moe_stage_weight(te_ref, w3_ref, w3b_ref)
    used = pl.program_id(1) < nu_ref[0]

    @pl.when(used)
    def _():
        x = x_ref[...]
        a = _dot(x, w1b_ref[...])
        o_ref[...] = (a * jax.nn.sigmoid(a) * _dot(x, w3b_ref[...])).astype(o_ref.dtype)

    @pl.when(jnp.logical_not(used))
    def _():
        o_ref[...] = jnp.zeros(o_ref.shape, o_ref.dtype)


def _moe_down_kernel(te_ref, nu_ref, x_ref, w_ref, o_ref, wb_ref):
    _moe_stage_weight(te_ref, w_ref, wb_ref)
    used = pl.program_id(1) < nu_ref[0]

    @pl.when(used)
    def _():
        o_ref[...] = _dot(x_ref[...], wb_ref[...])

    @pl.when(jnp.logical_not(used))
    def _():
        o_ref[...] = jnp.zeros(o_ref.shape, o_ref.dtype)


def _moe_wspec(layer, k, bn):
    return pl.BlockSpec((1, 1, k, bn), lambda j, t, te, nu: (layer, te[t], 0, j))


def moe_grouped_swiglu(xs, w1e, w3e, layer, tile_expert, n_used):
    n_rows, k = xs.shape
    n = w1e.shape[3]
    bn = _tile(n, 512, LANES)
    return pl.pallas_call(
        _moe_swiglu_kernel, out_shape=jax.ShapeDtypeStruct((n_rows, n), BF16),
        grid_spec=pltpu.PrefetchScalarGridSpec(
            num_scalar_prefetch=2, grid=(n // bn, n_rows // MOE_TILE),
            in_specs=[pl.BlockSpec((MOE_TILE, k), lambda j, t, te, nu: (t, 0)),
                      _moe_wspec(layer, k, bn), _moe_wspec(layer, k, bn)],
            out_specs=pl.BlockSpec((MOE_TILE, bn), lambda j, t, te, nu: (t, j)),
            scratch_shapes=[pltpu.VMEM((k, bn), BF16), pltpu.VMEM((k, bn), BF16)]),
        compiler_params=_params(("parallel", "arbitrary")), name="moe_swiglu",
    )(tile_expert, n_used, xs, w1e, w3e)


def moe_grouped_down(hm, w2e, layer, tile_expert, n_used):
    n_rows, k = hm.shape
    n = w2e.shape[3]
    bn = _tile(n, 1024, LANES)
    return pl.pallas_call(
        _moe_down_kernel, out_shape=jax.ShapeDtypeStruct((n_rows, n), F32),
        grid_spec=pltpu.PrefetchScalarGridSpec(
            num_scalar_prefetch=2, grid=(n // bn, n_rows // MOE_TILE),
            in_specs=[pl.BlockSpec((MOE_TILE, k), lambda j, t, te, nu: (t, 0)),
                      _moe_wspec(layer, k, bn)],
            out_specs=pl.BlockSpec((MOE_TILE, bn), lambda j, t, te, nu: (t, j)),
            scratch_shapes=[pltpu.VMEM((k, bn), BF16)]),
        compiler_params=_params(("parallel", "arbitrary")), name="moe_down",
    )(tile_expert, n_used, hm, w2e)


def _moe_combine_kernel(pos_ref, y_hbm, res_ref, route_ref, gate_ref, fg_ref, o_ref, buf, sems, *,
                        blocks_per_batch, n_batch, final_norm_out):
    i = pl.program_id(0)
    n_blocks = pl.num_programs(0)
    rows = res_ref.shape[0]

    def issue_block(blk, slot):
        base = blk * rows * TOP_K

        def issue(r, carry):
            for s in range(TOP_K):
                _row_copy(y_hbm, pos_ref[base + TOP_K * r + s], buf.at[slot, s, pl.ds(r, 1)],
                          sems.at[slot]).start()
            return carry

        lax.fori_loop(0, rows, issue, 0, unroll=DMA_LOOP_UNROLL)

    @pl.when(i == 0)
    def _():
        issue_block(0, 0)

    @pl.when(i + 1 < n_blocks)
    def _():
        issue_block(i + 1, (i + 1) % 2)

    slot = i % 2

    def wait(r, carry):
        for s in range(TOP_K):
            _row_copy(y_hbm, 0, buf.at[slot, s, pl.ds(r, 1)], sems.at[slot]).wait()
        return carry

    lax.fori_loop(0, rows, wait, 0, unroll=DMA_LOOP_UNROLL)
    w = route_ref[...]
    y = w[:, 0:1] * buf[slot, 0] + w[:, 1:2] * buf[slot, 1]
    r = _mod_row(i, blocks_per_batch, n_batch)
    x_new = res_ref[...] + gate_ref[pl.ds(r, 1), :] * y
    o_ref[...] = _rms(x_new, fg_ref[...]) if final_norm_out else x_new


def moe_combine(ys, pos, res, route, mod_l, which, rows, seq_rows, n_batch, final_g=None):
    d = ys.shape[1]
    bm = _tile(math.gcd(rows, seq_rows), MOE_COMBINE_ROWS, 8)
    kern = functools.partial(_moe_combine_kernel, blocks_per_batch=seq_rows // bm,
                             n_batch=n_batch, final_norm_out=final_g is not None)
    fg = jnp.ones((1, d), F32) if final_g is None else final_g.reshape(1, d)
    return pl.pallas_call(
        kern, out_shape=jax.ShapeDtypeStruct((rows, d), F32),
        grid_spec=pltpu.PrefetchScalarGridSpec(
            num_scalar_prefetch=1, grid=(rows // bm,),
            in_specs=[pl.BlockSpec(memory_space=pl.ANY),
                      pl.BlockSpec((bm, d), lambda i, p: (i, 0)),
                      pl.BlockSpec((bm, LANES), lambda i, p: (i, 0)),
                      pl.BlockSpec((COND_ROWS, d), lambda i, p: (0, which)),
                      pl.BlockSpec((1, d), lambda i, p: (0, 0))],
            out_specs=pl.BlockSpec((bm, d), lambda i, p: (i, 0)),
            scratch_shapes=[pltpu.VMEM((2, TOP_K, bm, d), F32), pltpu.SemaphoreType.DMA((2,))]),
        compiler_params=_params(("arbitrary",)), name="moe_combine",
    )(pos, ys, res, route, mod_l, fg)


def _merge_kernel(h_ref, f_ref, c_ref, a_ref, wg0, wg1, wg2, bg0, bg1, bg2, wf, wc, wa, o_ref,
                  wg0b, wg1b, wg2b, wfb, wcb, wab):
    for w_ref, wb_ref in ((wg0, wg0b), (wg1, wg1b), (wg2, wg2b), (wf, wfb), (wc, wcb), (wa, wab)):
        _stage_weight(w_ref, wb_ref)
    h = h_ref[...]

    def branch(wg, bg, x_ref, w):
        gate = jax.nn.sigmoid(_dot(h, wg[...]) + bg[0])
        return gate * _dot(x_ref[...], w[...])

    y = (branch(wg0b, bg0, f_ref, wfb) + branch(wg1b, bg1, c_ref, wcb)
         + branch(wg2b, bg2, a_ref, wab))
    o_ref[...] = y.astype(o_ref.dtype)


def merge_branches(h, four, conv, att, w_gate, b_gate, w_four_out, w_conv_out, w_attn_out, layer,
                   rows, seq_rows):
    d = h.shape[1]
    bm = _tile(math.gcd(rows, seq_rows), 512, BF16_SUBLANES)
    bn = _tile(d, 256, LANES)
    nb = d // bn
    depth = b_gate.shape[0]
    b3 = b_gate.reshape(depth, 1, 3 * d)

    def row_spec(width):
        return pl.BlockSpec((bm, width), lambda j, i: (i, 0))

    def col_spec(kdim, blk_off):
        return pl.BlockSpec((1, kdim, bn), lambda j, i: (layer, 0, blk_off + j),
                            pipeline_mode=pl.Buffered(1))

    kdims = (d, d, d, four.shape[1], conv.shape[1], att.shape[1])
    in_specs = ([row_spec(d), row_spec(four.shape[1]), row_spec(conv.shape[1]),
                 row_spec(att.shape[1])]
                + [col_spec(d, r * nb) for r in range(3)]
                + [col_spec(1, r * nb) for r in range(3)]
                + [col_spec(kd, 0) for kd in kdims[3:]])
    return pl.pallas_call(
        _merge_kernel, out_shape=jax.ShapeDtypeStruct((rows, d), BF16), grid=(nb, rows // bm),
        in_specs=in_specs, out_specs=pl.BlockSpec((bm, bn), lambda j, i: (i, j)),
        scratch_shapes=[pltpu.VMEM((kd, bn), BF16) for kd in kdims],
        compiler_params=_params(("parallel", "arbitrary")), name="merge_branches",
    )(h, four, conv, att, w_gate, w_gate, w_gate, b3, b3, b3, w_four_out, w_conv_out, w_attn_out)


def _fourier_tables(n_pos, group_dim, n2):
    n1 = n_pos // n2
    a = np.arange(n1)
    ang1 = 2.0 * np.pi * np.outer(a, a) / n1
    w1 = np.concatenate([np.cos(ang1), -np.sin(ang1)], axis=0)
    k = (np.arange(n1)[:, None] + n1 * np.arange(n2)[None, :])[:, :, None]
    ang2 = 2.0 * np.pi * ((k * np.arange(n2)[None, None, :]) % n_pos) / n_pos
    gc, gs = np.cos(ang2), np.sin(ang2)
    g = np.concatenate([np.concatenate([gc, gs], axis=2),
                        np.concatenate([-gs, gc], axis=2)], axis=1)
    c = np.arange(group_dim)
    angc = 2.0 * np.pi * np.outer(c, c) / group_dim
    scale = 1.0 / math.sqrt(n_pos * group_dim)
    as_bf16 = lambda t: jnp.asarray(t, dtype=F32).astype(BF16)
    return as_bf16(w1), as_bf16(g), as_bf16(np.cos(angc)), as_bf16(np.sin(angc)), scale


def _four1_kernel(w_ref, x_ref, o_ref):
    o_ref[0] = _dot(w_ref[...], x_ref[...]).astype(o_ref.dtype)


def _four2_kernel(*refs, scale, group_dim, complex_in):
    if complex_in:
        g_ref, zr_ref, zi_ref, cc_ref, sc_ref, o_ref = refs
        z = jnp.concatenate([zr_ref[0], zi_ref[0]], axis=0)
        g = g_ref[0]
    else:
        g_ref, zr_ref, cc_ref, sc_ref, o_ref = refs
        z = zr_ref[...]
        g = g_ref[0][:, :z.shape[0]]
    p = _dot(g, z)
    half = p.shape[0] // 2
    pr, pi = p[:half].astype(BF16), p[half:].astype(BF16)
    cc, sc = cc_ref[...], sc_ref[...]
    for grp in range(o_ref.shape[-1] // group_dim):
        sl = slice(grp * group_dim, (grp + 1) * group_dim)
        o = _dot(pr[:, sl], cc) + _dot(pi[:, sl], sc)
        o_ref[:, sl] = (o * scale).astype(o_ref.dtype)


def fourier_latent(f, n_batch, seq):
    total_rows = n_batch * seq
    fdim = f.shape[1]
    gd = fdim // FOURIER_GROUPS
    n2 = FFT_INNER
    n1 = seq // n2
    w1, g, cc, sc, scale = _fourier_tables(seq, gd, n2)
    wide = n2 * fdim
    bn = _tile(wide, 8192, LANES)
    z = pl.pallas_call(
        _four1_kernel, out_shape=jax.ShapeDtypeStruct((n_batch, 2 * n1, wide), BF16),
        grid=(n_batch, wide // bn),
        in_specs=[pl.BlockSpec((2 * n1, n1), lambda b, j: (0, 0)),
                  pl.BlockSpec((n1, bn), lambda b, j: (b, j))],
        out_specs=pl.BlockSpec((1, 2 * n1, bn), lambda b, j: (b, 0, j)),
        compiler_params=_params(("parallel", "parallel")), name="fourier_stage1",
    )(w1, f.reshape(f.shape[0] // n2, wide))
    z = z.reshape(n_batch * 2 * n1, n2, fdim)
    kern = functools.partial(_four2_kernel, scale=scale, group_dim=gd, complex_in=True)
    out = pl.pallas_call(
        kern, out_shape=jax.ShapeDtypeStruct((total_rows // n1, n1 * fdim), BF16),
        grid=(n_batch, n1),
        in_specs=[pl.BlockSpec((1, 2 * n2, 2 * n2), lambda b, k: (k, 0, 0)),
                  pl.BlockSpec((1, n2, fdim), lambda b, k: (b * 2 * n1 + k, 0, 0)),
                  pl.BlockSpec((1, n2, fdim), lambda b, k: (b * 2 * n1 + n1 + k, 0, 0)),
                  pl.BlockSpec((gd, gd), lambda b, k: (0, 0)),
                  pl.BlockSpec((gd, gd), lambda b, k: (0, 0))],
        out_specs=pl.BlockSpec((n2, fdim), lambda b, k: (b, k)),
        compiler_params=_params(("parallel", "parallel")), name="fourier_stage2",
    )(g, z, z, cc, sc)
    return out.reshape(total_rows, fdim)


def fourier_context(f, n_batch, ctx_len, lat_rows):
    fdim = f.shape[1]
    gd = fdim // FOURIER_GROUPS
    n2 = ctx_len
    _, g, cc, sc, scale = _fourier_tables(ctx_len, gd, n2)
    first = lat_rows // ctx_len
    kern = functools.partial(_four2_kernel, scale=scale, group_dim=gd, complex_in=False)
    return pl.pallas_call(
        kern, out_shape=jax.ShapeDtypeStruct((n_batch * ctx_len, fdim), BF16), grid=(n_batch,),
        in_specs=[pl.BlockSpec((1, 2 * n2, 2 * n2), lambda b: (0, 0, 0)),
                  pl.BlockSpec((ctx_len, fdim), lambda b: (first + b, 0)),
                  pl.BlockSpec((gd, gd), lambda b: (0, 0)),
                  pl.BlockSpec((gd, gd), lambda b: (0, 0))],
        out_specs=pl.BlockSpec((ctx_len, fdim), lambda b: (b, 0)),
        compiler_params=_params(("parallel",)), name="fourier_context",
    )(g, f, cc, sc)


F32_SUBLANES = 8


def _conv_kernel(prev_ref, cur_ref, next_ref, w_ref, b_ref, g_ref, bb_ref, o_ref, z_ref, zs_ref,
                 *, starts, ends, chunk):
    i = pl.program_id(0)
    ts, c = o_ref.shape

    def glu(ref):
        v = ref[...].astype(F32)
        return v[:, :c] * jax.nn.sigmoid(v[:, c:])

    def any_eq(vals):
        hit = i == vals[0]
        for v in vals[1:]:
            hit = jnp.logical_or(hit, i == v)
        return hit

    z_ref[0:HALO_ROWS, :] = jnp.where(any_eq(starts), 0.0, glu(prev_ref))
    z_ref[HALO_ROWS:HALO_ROWS + ts, :] = glu(cur_ref)
    z_ref[HALO_ROWS + ts:, :] = jnp.where(any_eq(ends), 0.0, glu(next_ref))
    n_shift = zs_ref.shape[1]
    for b in range(1, F32_SUBLANES):
        zs_ref[b - 1] = z_ref[b:b + n_shift, :]
    bias, gamma, beta = b_ref[...], g_ref[...], bb_ref[...]
    for r0 in range(0, ts, chunk):
        acc = jnp.zeros((chunk, c), F32)
        for t in range(CONV_WIDTH):
            lo = r0 + t + HALO_ROWS - CONV_PAD
            b = lo % F32_SUBLANES
            src = z_ref if b == 0 else zs_ref.at[b - 1]
            acc = acc + w_ref[t:t + 1, :] * src[lo - b:lo - b + chunk, :]
        acc = acc + bias
        mu = jnp.mean(acc, axis=-1, keepdims=True)
        dev = acc - mu
        var = jnp.mean(dev * dev, axis=-1, keepdims=True)
        y = dev * lax.rsqrt(var + LN_EPS) * gamma + beta
        o_ref[r0:r0 + chunk, :] = (y * jax.nn.sigmoid(y)).astype(o_ref.dtype)


def conformer_conv(cab, w_dw, b_dw, ln_g, ln_b, rows, seq_lens):
    c = cab.shape[1] // 2
    ts = _tile(math.gcd(*seq_lens) if len(seq_lens) > 1 else seq_lens[0], 256, HALO_ROWS)
    per = ts // HALO_ROWS
    starts, ends, pos = [], [], 0
    for n in seq_lens:
        starts.append(pos // ts)
        pos += n
        ends.append(pos // ts - 1)
    assert pos == rows
    last_halo = rows // HALO_ROWS - 1
    kern = functools.partial(_conv_kernel, starts=tuple(starts), ends=tuple(ends), chunk=16)
    vec = lambda v: v.reshape(1, c)
    vspec = pl.BlockSpec((1, c), lambda i: (0, 0))
    return pl.pallas_call(
        kern, out_shape=jax.ShapeDtypeStruct((rows, c), BF16), grid=(rows // ts,),
        in_specs=[pl.BlockSpec((HALO_ROWS, 2 * c), lambda i: (jnp.maximum(i * per - 1, 0), 0)),
                  pl.BlockSpec((ts, 2 * c), lambda i: (i, 0)),
                  pl.BlockSpec((HALO_ROWS, 2 * c),
                               lambda i: (jnp.minimum((i + 1) * per, last_halo), 0)),
                  pl.BlockSpec((CONV_WIDTH, c), lambda i: (0, 0)), vspec, vspec, vspec],
        out_specs=pl.BlockSpec((ts, c), lambda i: (i, 0)),
        scratch_shapes=[pltpu.VMEM((ts + 2 * HALO_ROWS, c), F32),
                        pltpu.VMEM((F32_SUBLANES - 1, ts + 2 * HALO_ROWS - F32_SUBLANES, c), F32)],
        compiler_params=_params(("parallel",)), name="conformer_conv",
    )(cab, cab, cab, w_dw, vec(b_dw), vec(ln_g), vec(ln_b))


ATTN_ROW_GROUP = 64


def _lane_tile(x, reps):
    return x if reps == 1 else jnp.concatenate([x] * reps, axis=1)


def _attn_kernel(*refs, lam_init, tk, has_latent):
    if has_latent:
        (q_ref, k_ref, v_ref, kc_ref, vc_ref, lam_ref, g_ref, o_ref,
         m_ref, l_ref, a_ref, acc_ref, s_ref, p_ref) = refs
    else:
        (q_ref, kc_ref, vc_ref, lam_ref, g_ref, o_ref,
         m_ref, l_ref, a_ref, acc_ref, s_ref, p_ref) = refs
    hd = ATTN_HEAD_DIM
    m_ref[...] = jnp.full(m_ref.shape, -jnp.inf, F32)
    l_ref[...] = jnp.zeros(l_ref.shape, F32)
    acc_ref[...] = jnp.zeros(acc_ref.shape, F32)

    def scores(slot, k):
        n = k.shape[0]
        for mp in range(2):
            s_ref[slot, mp, :, :n] = lax.dot_general(
                q_ref[:, mp * hd:(mp + 1) * hd], k[:, mp * hd:(mp + 1) * hd],
                (((1,), (1,)), ((), ())), preferred_element_type=F32)

    def softmax_pv(slot, v):
        n = v.shape[0]
        tq = q_ref.shape[0]
        sub = min(ATTN_ROW_GROUP, tq)
        for mp in range(2):
            for r0 in range(0, tq, sub):
                rows = slice(r0, r0 + sub)
                s = s_ref[slot, mp, rows, :n]
                m_prev = m_ref[mp, rows]
                m_new = jnp.maximum(m_prev, jnp.max(s, axis=-1, keepdims=True))
                alpha = jnp.exp2(m_prev - m_new)
                p = jnp.exp2(s - _lane_tile(m_new, n // LANES))
                l_ref[mp, rows] = alpha * l_ref[mp, rows] + jnp.sum(p, axis=-1, keepdims=True)
                m_ref[mp, rows] = m_new
                a_ref[mp, rows] = alpha
                p_ref[slot, mp, rows, :n] = p.astype(BF16)
            acc_ref[mp] = (acc_ref[mp] * _lane_tile(a_ref[mp], ATTN_V_DIM // LANES)
                           + _dot(p_ref[slot, mp, :, :n], v))

    if has_latent:
        n_chunks = k_ref.shape[0] // tk
        assert n_chunks >= 2 and n_chunks % 2 == 0

        def chunk(ref, c):
            off = c * tk if isinstance(c, int) else pl.multiple_of(c * tk, tk)
            return ref[pl.ds(off, tk), :]

        scores(0, chunk(k_ref, 0))

        def body(i, carry):
            c = 2 * i
            scores(1, chunk(k_ref, c + 1))
            softmax_pv(0, chunk(v_ref, c))
            scores(0, chunk(k_ref, c + 2))
            softmax_pv(1, chunk(v_ref, c + 1))
            return carry

        lax.fori_loop(0, n_chunks // 2 - 1, body, 0)
        c = n_chunks - 2
        scores(1, chunk(k_ref, c + 1))
        softmax_pv(0, chunk(v_ref, c))
        scores(0, kc_ref[...])
        softmax_pv(1, chunk(v_ref, c + 1))
        softmax_pv(0, vc_ref[...])
    else:
        scores(0, kc_ref[...])
        softmax_pv(0, vc_ref[...])

    lam_v = lam_ref[...]
    lam = (jnp.exp(jnp.sum(lam_v[0:1] * lam_v[1:2], axis=-1, keepdims=True))
           - jnp.exp(jnp.sum(lam_v[2:3] * lam_v[3:4], axis=-1, keepdims=True)) + lam_init)
    reps = ATTN_V_DIM // LANES
    o = (acc_ref[0] * _lane_tile(1.0 / l_ref[0], reps)
         - lam * (acc_ref[1] * _lane_tile(1.0 / l_ref[1], reps)))
    o = _rms(o, g_ref[...]) * (1.0 - lam_init)
    o_ref[...] = o.astype(o_ref.dtype)


def _attn_scratch(tq, tk, slots):
    stat = pltpu.VMEM((2, tq, LANES), F32)
    return [stat, stat, stat, pltpu.VMEM((2, tq, ATTN_V_DIM), F32),
            pltpu.VMEM((slots, 2, tq, tk), F32), pltpu.VMEM((slots, 2, tq, tk), BF16)]


def attention_latent(qkv, lam_vecs, subln_g, lam_init, n_batch, seq, ctx_len):
    total_rows = n_batch * seq
    tq = _tile(seq, 512, BF16_SUBLANES)
    tk = _tile(seq, 1024, LANES)
    qb = seq // tq
    kblk = ATTN_QK_DIM // ATTN_V_DIM
    vblk = 2 * ATTN_QK_DIM // ATTN_V_DIM
    first_ctx = n_batch * seq // ctx_len
    kern = functools.partial(_attn_kernel, lam_init=lam_init, tk=tk, has_latent=True)
    w = ATTN_V_DIM
    return pl.pallas_call(
        kern, out_shape=jax.ShapeDtypeStruct((total_rows, ATTN_DIM), BF16),
        grid=(n_batch, ATTN_HEADS, qb),
        in_specs=[pl.BlockSpec((tq, w), lambda b, h, i: (b * qb + i, h)),
                  pl.BlockSpec((seq, w), lambda b, h, i: (b, kblk + h)),
                  pl.BlockSpec((seq, w), lambda b, h, i: (b, vblk + h)),
                  pl.BlockSpec((ctx_len, w), lambda b, h, i: (first_ctx + b, kblk + h)),
                  pl.BlockSpec((ctx_len, w), lambda b, h, i: (first_ctx + b, vblk + h)),
                  pl.BlockSpec((4, ATTN_HEAD_DIM), lambda b, h, i: (0, 0)),
                  pl.BlockSpec((1, w), lambda b, h, i: (0, 0))],
        out_specs=pl.BlockSpec((tq, w), lambda b, h, i: (b * qb + i, h)),
        scratch_shapes=_attn_scratch(tq, max(tk, ctx_len), 2),
        compiler_params=_params(("parallel", "parallel", "parallel")), name="attention_latent",
    )(qkv, qkv, qkv, qkv, qkv, lam_vecs, subln_g.reshape(1, w))


def attention_context(qkv, lam_vecs, subln_g, lam_init, n_batch, ctx_len, lat_rows):
    kblk = ATTN_QK_DIM // ATTN_V_DIM
    vblk = 2 * ATTN_QK_DIM // ATTN_V_DIM
    first = lat_rows // ctx_len
    kern = functools.partial(_attn_kernel, lam_init=lam_init, tk=ctx_len, has_latent=False)
    w = ATTN_V_DIM
    return pl.pallas_call(
        kern, out_shape=jax.ShapeDtypeStruct((n_batch * ctx_len, ATTN_DIM), BF16),
        grid=(n_batch, ATTN_HEADS),
        in_specs=[pl.BlockSpec((ctx_len, w), lambda b, h: (first + b, h)),
                  pl.BlockSpec((ctx_len, w), lambda b, h: (first + b, kblk + h)),
                  pl.BlockSpec((ctx_len, w), lambda b, h: (first + b, vblk + h)),
                  pl.BlockSpec((4, ATTN_HEAD_DIM), lambda b, h: (0, 0)),
                  pl.BlockSpec((1, w), lambda b, h: (0, 0))],
        out_specs=pl.BlockSpec((ctx_len, w), lambda b, h: (b, h)),
        scratch_shapes=_attn_scratch(ctx_len, ctx_len, 1),
        compiler_params=_params(("parallel", "parallel")), name="attention_context",
    )(qkv, qkv, qkv, lam_vecs, subln_g.reshape(1, w))


def _rope_tables(n):
    pos = np.arange(n)
    inv = ROPE_BASE ** (-np.arange(ROPE_FREQS, dtype=np.float64) / ROPE_FREQS)
    ang = [(pos // GRID_W)[:, None] * inv, (pos % GRID_W)[:, None] * inv]
    zero = np.zeros((n, ROPE_FREQS))
    cos = np.concatenate([np.cos(ang[0])] * 2 + [np.cos(ang[1])] * 2, axis=1)
    s_up = np.concatenate([zero, np.sin(ang[0]), zero, np.sin(ang[1])], axis=1)
    s_dn = np.concatenate([-np.sin(ang[0]), zero, -np.sin(ang[1]), zero], axis=1)
    return tuple(jnp.asarray(t, dtype=F32) for t in (cos, s_up, s_dn))


def kernel(x, c, ctx, c_ctx, norm1_g, w_mod, b_mod, w_in, w_gate, b_gate, w_four_out, w_dw, b_dw,
           conv_ln_g, conv_ln_b, w_conv_out, lam_q1, lam_k1, lam_q2, lam_k2, subln_g, w_attn_out,
           w_o, norm2_g, w1, w3, w2, w_router, b_router, w1e, w3e, w2e, norm_f_g):
    n_batch, seq, d = x.shape
    ctx_len = ctx.shape[1]
    depth = w_in.shape[0]
    lat_rows, ctx_rows = n_batch * seq, n_batch * ctx_len
    all_rows = lat_rows + ctx_rows
    fdim = w_four_out.shape[1]
    cdim = w_conv_out.shape[1]
    assert n_batch < COND_ROWS and ctx_len % HALO_ROWS == 0 and seq % FFT_INNER == 0

    xa = jnp.concatenate([x.reshape(lat_rows, d), ctx.reshape(ctx_rows, d)], axis=0)
    cond = jnp.zeros((COND_ROWS, d), F32).at[:n_batch].set(c).at[n_batch].set(c_ctx)
    mod = ada_params_all(cond, w_mod, b_mod)
    rope = _rope_tables(seq)

    for l in range(depth):
        last = l == depth - 1
        rows = lat_rows if last else all_rows
        seqs = [seq] * n_batch + ([] if last else [ctx_len] * n_batch)
        lam_init = 0.8 - 0.6 * math.exp(-0.3 * l)
        lam_vecs = jnp.stack([lam_q1[l], lam_k1[l], lam_q2[l], lam_k2[l]]).astype(F32)

        h = norm_modulate(xa, norm1_g[l], mod[l], 0, all_rows, seq, n_batch)
        f = mm_plain(h, w_in, l, 0, fdim, rows, seq)
        cab = mm_plain(h, w_in, l, fdim, 2 * cdim, rows, seq)
        qkv = mm_qkv_rope(h, w_in, l, fdim + 2 * cdim, all_rows, lat_rows, seq, rope)
        four = fourier_latent(f, n_batch, seq)
        att = attention_latent(qkv, lam_vecs, subln_g[l], lam_init, n_batch, seq, ctx_len)
        if not last:
            four = jnp.concatenate([four, fourier_context(f, n_batch, ctx_len, lat_rows)], axis=0)
            att = jnp.concatenate([att, attention_context(qkv, lam_vecs, subln_g[l], lam_init,
                                                          n_batch, ctx_len, lat_rows)], axis=0)
        conv = conformer_conv(cab, w_dw[l], b_dw[l], conv_ln_g[l], conv_ln_b[l], rows, seqs)
        y = merge_branches(h, four, conv, att, w_gate, b_gate, w_four_out, w_conv_out,
                           w_attn_out, l, rows, seq)
        xa = mm_residual(y, w_o, l, xa, mod[l], 2, rows, seq, n_batch)

        i = l // 2
        if l % 2 == 0:
            h2 = norm_modulate(xa, norm2_g[l], mod[l], 3, rows, seq, n_batch)
            hm = mm_swiglu(h2, w1, w3, i, rows, seq)
            xa = mm_residual(hm, w2, i, xa, mod[l], 5, rows, seq, n_batch)
        else:
            h2p, route = norm_modulate(xa, norm2_g[l], mod[l], 3, rows, seq, n_batch,
                                       router=(w_router[i], b_router[i]))
            row_token, pos, tile_expert, n_used = moe_plan(route, w1e.shape[1])
            xs = moe_gather(h2p, row_token)
            hm = moe_grouped_swiglu(xs, w1e, w3e, i, tile_expert, n_used)
            ys = moe_grouped_down(hm, w2e, i, tile_expert, n_used)
            xa = moe_combine(ys, pos, xa, route, mod[l], 5, rows, seq, n_batch,
                             final_g=norm_f_g if last else None)
            if last:
                return xa.reshape(n_batch, seq, d)

    return final_norm(xa, norm_f_g, lat_rows).reshape(n_batch, seq, d)
```

```python
import functools
import math

import numpy as np
import jax
import jax.numpy as jnp
from jax import lax
from jax.experimental import pallas as pl
from jax.experimental.pallas import tpu as pltpu

F32 = jnp.float32
BF16 = jnp.bfloat16

GRID_W = 64
EPS = 1e-6
LN_EPS = 1e-5
N_MOD = 6
FOURIER_GROUPS = 4
CONV_WIDTH = 31
CONV_PAD = CONV_WIDTH // 2
ATTN_HEADS = 8
ATTN_HEAD_DIM = 128
ATTN_V_DIM = 2 * ATTN_HEAD_DIM
ATTN_QK_DIM = ATTN_HEADS * 2 * ATTN_HEAD_DIM
ATTN_DIM = ATTN_HEADS * ATTN_V_DIM
ROPE_BASE = 10000.0
ROPE_FREQS = ATTN_HEAD_DIM // 4
N_EXPERTS = 8
TOP_K = 2

LANES = 128
BF16_SUBLANES = 16
V7X_VMEM_LIMIT_BYTES = 56 * 1024 * 1024
FFT_INNER = 128
COND_ROWS = 8
HALO_ROWS = BF16_SUBLANES


def _tile(n, pref, align):
    t = (min(pref, n) // align) * align
    while t > align and n % t:
        t -= align
    assert t >= align and n % t == 0, (n, pref, align)
    return t


def _params(semantics):
    return pltpu.CompilerParams(dimension_semantics=semantics,
                                vmem_limit_bytes=V7X_VMEM_LIMIT_BYTES)


def _dot(a, b):
    return jnp.dot(a, b, preferred_element_type=F32)


def _ada_kernel(c_ref, w_ref, b_ref, o_ref):
    c = c_ref[...]
    s = (c * jax.nn.sigmoid(c)).astype(BF16)
    o_ref[0] = _dot(s, w_ref[0].astype(BF16)) + b_ref[0]


def ada_params_all(cond, w_mod, b_mod):
    depth, d, n = w_mod.shape
    bn = _tile(n, 512, LANES)
    return pl.pallas_call(
        _ada_kernel,
        out_shape=jax.ShapeDtypeStruct((depth, COND_ROWS, n), F32),
        grid=(depth, n // bn),
        in_specs=[pl.BlockSpec((COND_ROWS, d), lambda l, j: (0, 0)),
                  pl.BlockSpec((1, d, bn), lambda l, j: (l, 0, j)),
                  pl.BlockSpec((1, 1, bn), lambda l, j: (l, 0, j))],
        out_specs=pl.BlockSpec((1, COND_ROWS, bn), lambda l, j: (l, 0, j)),
        compiler_params=_params(("parallel", "parallel")),
        name="ada_params",
    )(cond, w_mod, b_mod.reshape(depth, 1, n))


def _mod_row(i, blocks_per_batch, n_batch):
    return jnp.minimum(i // blocks_per_batch, n_batch)


def _rms(x, g):
    ms = jnp.mean(x * x, axis=-1, keepdims=True)
    return x * lax.rsqrt(ms + EPS) * g


def _normmod_kernel(x_ref, g_ref, sh_ref, sc_ref, o_ref, *, blocks_per_batch, n_batch):
    r = _mod_row(pl.program_id(0), blocks_per_batch, n_batch)
    y = _rms(x_ref[...], g_ref[...])
    h = y * (1.0 + sc_ref[pl.ds(r, 1), :]) + sh_ref[pl.ds(r, 1), :]
    o_ref[...] = h.astype(o_ref.dtype)


def _normmod_router_kernel(x_ref, g_ref, sh_ref, sc_ref, wr_ref, br_ref, o_ref, route_ref, *,
                           blocks_per_batch, n_batch):
    r = _mod_row(pl.program_id(0), blocks_per_batch, n_batch)
    y = _rms(x_ref[...], g_ref[...])
    h = y * (1.0 + sc_ref[pl.ds(r, 1), :]) + sh_ref[pl.ds(r, 1), :]
    bits = lax.bitcast_convert_type(h.astype(BF16).astype(F32), jnp.uint32)
    half = bits.shape[1] // 2
    o_ref[...] = bits[:, :half] | (bits[:, half:] >> 16)
    logits = jnp.dot(h, wr_ref[...], preferred_element_type=F32,
                     precision=lax.Precision.HIGHEST) + br_ref[...]
    lane = lax.broadcasted_iota(jnp.int32, logits.shape, 1)
    v1 = jnp.max(logits, axis=-1, keepdims=True)
    i1 = jnp.min(jnp.where(logits == v1, lane, LANES), axis=-1, keepdims=True)
    rest = jnp.where(lane == i1, -jnp.inf, logits)
    v2 = jnp.max(rest, axis=-1, keepdims=True)
    i2 = jnp.min(jnp.where(rest == v2, lane, LANES), axis=-1, keepdims=True)
    e = jnp.exp(v2 - v1)
    w1 = 1.0 / (1.0 + e)
    route_ref[...] = (jnp.where(lane == 0, w1, 0.0) + jnp.where(lane == 1, e * w1, 0.0)
                      + jnp.where(lane == 2, i1.astype(F32), 0.0)
                      + jnp.where(lane == 3, i2.astype(F32), 0.0))


def _final_norm_kernel(x_ref, g_ref, o_ref):
    o_ref[...] = _rms(x_ref[...], g_ref[...])


def norm_modulate(xa, g, mod_l, which, rows, seq_rows, n_batch, router=None):
    d = xa.shape[1]
    bm = _tile(math.gcd(rows, seq_rows), 256, BF16_SUBLANES)
    kw = dict(blocks_per_batch=seq_rows // bm, n_batch=n_batch)
    in_specs = [pl.BlockSpec((bm, d), lambda i: (i, 0)),
                pl.BlockSpec((1, d), lambda i: (0, 0)),
                pl.BlockSpec((COND_ROWS, d), lambda i: (0, which)),
                pl.BlockSpec((COND_ROWS, d), lambda i: (0, which + 1))]
    args = [xa, g.reshape(1, d), mod_l, mod_l]
    h_shape = jax.ShapeDtypeStruct((rows, d), BF16)
    h_spec = pl.BlockSpec((bm, d), lambda i: (i, 0))
    if router is None:
        return pl.pallas_call(
            functools.partial(_normmod_kernel, **kw), out_shape=h_shape, grid=(rows // bm,),
            in_specs=in_specs, out_specs=h_spec, compiler_params=_params(("parallel",)),
            name="norm_modulate")(*args)
    w_r, b_r = router
    ne = w_r.shape[1]
    w_pad = jnp.zeros((d, LANES), F32).at[:, :ne].set(w_r)
    b_pad = jnp.full((1, LANES), -jnp.inf, F32).at[0, :ne].set(b_r)
    return pl.pallas_call(
        functools.partial(_normmod_router_kernel, **kw),
        out_shape=(jax.ShapeDtypeStruct((rows, d // 2), jnp.uint32),
                   jax.ShapeDtypeStruct((rows, LANES), F32)), grid=(rows // bm,),
        in_specs=in_specs + [pl.BlockSpec((d, LANES), lambda i: (0, 0)),
                             pl.BlockSpec((1, LANES), lambda i: (0, 0))],
        out_specs=(pl.BlockSpec((bm, d // 2), lambda i: (i, 0)),
                   pl.BlockSpec((bm, LANES), lambda i: (i, 0))),
        compiler_params=_params(("parallel",)), name="norm_modulate_router")(*args, w_pad, b_pad)


def final_norm(xa, g, rows):
    d = xa.shape[1]
    bm = _tile(rows, 256, 8)
    return pl.pallas_call(
        _final_norm_kernel, out_shape=jax.ShapeDtypeStruct((rows, d), F32), grid=(rows // bm,),
        in_specs=[pl.BlockSpec((bm, d), lambda i: (i, 0)), pl.BlockSpec((1, d), lambda i: (0, 0))],
        out_specs=pl.BlockSpec((bm, d), lambda i: (i, 0)),
        compiler_params=_params(("parallel",)), name="final_norm")(xa, g.reshape(1, d))


def _stage_weight(w_ref, wb_ref):
    @pl.when(pl.program_id(1) == 0)
    def _():
        wb_ref[...] = w_ref[0].astype(BF16)


def _wspec(layer, k, bn, col_blk_off=0, single_buffer=False):
    return pl.BlockSpec((1, k, bn), lambda j, i: (layer, 0, col_blk_off + j),
                        pipeline_mode=pl.Buffered(1) if single_buffer else None)


def _mm_plain_kernel(x_ref, w_ref, o_ref, wb_ref):
    _stage_weight(w_ref, wb_ref)
    o_ref[...] = _dot(x_ref[...], wb_ref[...]).astype(o_ref.dtype)


def _mm_rope_kernel(x_ref, w_ref, c_ref, s1_ref, s2_ref, o_ref, wb_ref, *, rope_cols):
    j = pl.program_id(0)
    _stage_weight(w_ref, wb_ref)

    @pl.when(j < rope_cols)
    def _():
        acc = _dot(x_ref[...], wb_ref[...])
        c, s1, s2 = c_ref[...], s1_ref[...], s2_ref[...]
        for g in range(acc.shape[1] // LANES):
            t = acc[:, g * LANES:(g + 1) * LANES]
            r = (t * c + pltpu.roll(t, ROPE_FREQS, 1) * s1
                 + pltpu.roll(t, LANES - ROPE_FREQS, 1) * s2)
            o_ref[:, g * LANES:(g + 1) * LANES] = r.astype(o_ref.dtype)

    @pl.when(j >= rope_cols)
    def _():
        o_ref[...] = _dot(x_ref[...], wb_ref[...]).astype(o_ref.dtype)


def _mm_res_kernel(x_ref, w_ref, res_ref, gate_ref, o_ref, wb_ref, *, blocks_per_batch, n_batch):
    r = _mod_row(pl.program_id(1), blocks_per_batch, n_batch)
    _stage_weight(w_ref, wb_ref)
    o_ref[...] = res_ref[...] + gate_ref[pl.ds(r, 1), :] * _dot(x_ref[...], wb_ref[...])


def _mm_swiglu_kernel(x_ref, w1_ref, w3_ref, o_ref, w1b_ref, w3b_ref):
    _stage_weight(w1_ref, w1b_ref)
    _stage_weight(w3_ref, w3b_ref)
    x = x_ref[...]
    a = _dot(x, w1b_ref[...])
    o_ref[...] = (a * jax.nn.sigmoid(a) * _dot(x, w3b_ref[...])).astype(o_ref.dtype)


def _mm_tiles(rows, seq_rows, n, bn_pref=512):
    bm = _tile(math.gcd(rows, seq_rows), 512, BF16_SUBLANES)
    bn = _tile(n, bn_pref, LANES)
    return bm, bn


def mm_plain(x, w, layer, col_off, n, rows, seq_rows):
    k = x.shape[1]
    bm, bn = _mm_tiles(rows, seq_rows, math.gcd(n, col_off) if col_off else n, 1024)
    return pl.pallas_call(
        _mm_plain_kernel, out_shape=jax.ShapeDtypeStruct((rows, n), BF16),
        grid=(n // bn, rows // bm),
        in_specs=[pl.BlockSpec((bm, k), lambda j, i: (i, 0)), _wspec(layer, k, bn, col_off // bn)],
        out_specs=pl.BlockSpec((bm, bn), lambda j, i: (i, j)),
        scratch_shapes=[pltpu.VMEM((k, bn), BF16)],
        compiler_params=_params(("parallel", "arbitrary")), name="mm_plain")(x, w)


def mm_qkv_rope(x, w, layer, col_off, rows, lat_rows, seq, tables):
    k = x.shape[1]
    n = 2 * ATTN_QK_DIM + ATTN_DIM
    bm = _tile(math.gcd(rows, seq), 512, BF16_SUBLANES)
    bn = _tile(math.gcd(ATTN_QK_DIM, col_off), 1024, LANES)
    seq_blocks, lat_blocks, q_cols = seq // bm, lat_rows // bm, ATTN_QK_DIM // bn
    scale = ATTN_HEAD_DIM ** -0.5 * math.log2(math.e)
    cos, s_up, s_dn = tables
    zeros = jnp.zeros((bm, LANES), F32)
    cos_all = jnp.concatenate([cos * scale, cos, zeros + scale, zeros + 1.0], axis=0)
    sup_all = jnp.concatenate([s_up * scale, s_up, zeros, zeros], axis=0)
    sdn_all = jnp.concatenate([s_dn * scale, s_dn, zeros, zeros], axis=0)

    def tab_block(j, i):
        is_q = j < q_cols
        lat = jnp.where(is_q, 0, seq_blocks) + i % seq_blocks
        ctx = jnp.where(is_q, 2 * seq_blocks, 2 * seq_blocks + 1)
        return jnp.where(i < lat_blocks, lat, ctx), 0

    tab_spec = pl.BlockSpec((bm, LANES), tab_block)
    kern = functools.partial(_mm_rope_kernel, rope_cols=2 * q_cols)
    return pl.pallas_call(
        kern, out_shape=jax.ShapeDtypeStruct((rows, n), BF16), grid=(n // bn, rows // bm),
        in_specs=[pl.BlockSpec((bm, k), lambda j, i: (i, 0)),
                  _wspec(layer, k, bn, col_off // bn, single_buffer=True),
                  tab_spec, tab_spec, tab_spec],
        out_specs=pl.BlockSpec((bm, bn), lambda j, i: (i, j)),
        scratch_shapes=[pltpu.VMEM((k, bn), BF16)],
        compiler_params=_params(("parallel", "arbitrary")), name="mm_qkv_rope",
    )(x, w, cos_all, sup_all, sdn_all)


def mm_residual(x, w, layer, res, mod_l, which, rows, seq_rows, n_batch):
    _, k, n = w.shape
    bm, bn = _mm_tiles(rows, seq_rows, n, 1024)
    wide_bytes = k * bn * (4 + 2) + 2 * bm * k * 2 + 4 * bm * bn * 4
    if wide_bytes <= V7X_VMEM_LIMIT_BYTES * 3 // 4:
        wspec = pl.BlockSpec((1, k, bn), lambda j, i: (layer, 0, j), pipeline_mode=pl.Buffered(1))
    else:
        bm, bn = _mm_tiles(rows, seq_rows, n)
        wspec = _wspec(layer, k, bn)
    gate_blk = which * (n // bn)
    kern = functools.partial(_mm_res_kernel, blocks_per_batch=seq_rows // bm, n_batch=n_batch)
    return pl.pallas_call(
        kern, out_shape=jax.ShapeDtypeStruct((rows, n), F32), grid=(n // bn, rows // bm),
        in_specs=[pl.BlockSpec((bm, k), lambda j, i: (i, 0)), wspec,
                  pl.BlockSpec((bm, bn), lambda j, i: (i, j)),
                  pl.BlockSpec((COND_ROWS, bn), lambda j, i: (0, gate_blk + j))],
        out_specs=pl.BlockSpec((bm, bn), lambda j, i: (i, j)),
        scratch_shapes=[pltpu.VMEM((k, bn), BF16)],
        compiler_params=_params(("parallel", "arbitrary")), name="mm_residual")(x, w, res, mod_l)


def mm_swiglu(x, w1, w3, layer, rows, seq_rows):
    _, k, n = w1.shape
    bm, bn = _mm_tiles(rows, seq_rows, n)
    return pl.pallas_call(
        _mm_swiglu_kernel, out_shape=jax.ShapeDtypeStruct((rows, n), BF16),
        grid=(n // bn, rows // bm),
        in_specs=[pl.BlockSpec((bm, k), lambda j, i: (i, 0)), _wspec(layer, k, bn),
                  _wspec(layer, k, bn)],
        out_specs=pl.BlockSpec((bm, bn), lambda j, i: (i, j)),
        scratch_shapes=[pltpu.VMEM((k, bn), BF16), pltpu.VMEM((k, bn), BF16)],
        compiler_params=_params(("parallel", "arbitrary")), name="mm_swiglu")(x, w1, w3)


MOE_TILE = 512
MOE_GATHER_ROWS = 256
MOE_COMBINE_ROWS = 128
DMA_LOOP_UNROLL = 8


def moe_plan(route, n_experts):
    n_tok = route.shape[0]
    n_pairs = TOP_K * n_tok
    e_pair = route[:, 2:2 + TOP_K].astype(jnp.int32).reshape(n_pairs)
    onehot = (e_pair[:, None] == jnp.arange(n_experts, dtype=jnp.int32)[None, :])
    csum = jnp.cumsum(onehot.astype(jnp.int32), axis=0)
    rank = jnp.take_along_axis(csum, e_pair[:, None], axis=1)[:, 0] - 1
    counts = csum[-1]
    tiles_per = (counts + MOE_TILE - 1) // MOE_TILE
    tile_end = jnp.cumsum(tiles_per)
    pos = (tile_end - tiles_per)[e_pair] * MOE_TILE + rank
    n_rows = n_pairs + n_experts * MOE_TILE
    n_tiles = n_rows // MOE_TILE
    row_token = jnp.zeros((n_rows,), jnp.int32).at[pos].set(
        jnp.arange(n_pairs, dtype=jnp.int32) // TOP_K)
    tile_expert = jnp.minimum(
        jnp.searchsorted(tile_end, jnp.arange(n_tiles, dtype=jnp.int32), side="right"),
        n_experts - 1).astype(jnp.int32)
    return row_token, pos.astype(jnp.int32), tile_expert, tile_end[-1:].astype(jnp.int32)


def _row_copy(src_hbm, row, dst, sem):
    return pltpu.make_async_copy(src_hbm.at[pl.ds(row, 1)], dst, sem)


def _moe_gather_kernel(tok_ref, h_hbm, xs_hbm, sems, *, rows):
    i = pl.program_id(0)
    n_blocks = pl.num_programs(0)

    def dst(blk, r):
        return xs_hbm.at[pl.ds(blk * rows + r, 1)]

    def issue(r, carry):
        _row_copy(h_hbm, tok_ref[i * rows + r], dst(i, r), sems.at[i % 2]).start()
        return carry

    lax.fori_loop(0, rows, issue, 0, unroll=DMA_LOOP_UNROLL)

    def wait_block(blk):
        def wait(r, carry):
            _row_copy(h_hbm, 0, dst(blk, r), sems.at[blk % 2]).wait()
            return carry

        lax.fori_loop(0, rows, wait, 0, unroll=DMA_LOOP_UNROLL)

    @pl.when(i > 0)
    def _():
        wait_block(i - 1)

    @pl.when(i == n_blocks - 1)
    def _():
        wait_block(i)


def moe_gather(h_packed, row_token):
    n_rows = row_token.shape[0]
    rows = MOE_GATHER_ROWS
    return pl.pallas_call(
        functools.partial(_moe_gather_kernel, rows=rows),
        out_shape=jax.ShapeDtypeStruct((n_rows, h_packed.shape[1]), jnp.uint32),
        grid_spec=pltpu.PrefetchScalarGridSpec(
            num_scalar_prefetch=1, grid=(n_rows // rows,),
            in_specs=[pl.BlockSpec(memory_space=pl.ANY)],
            out_specs=pl.BlockSpec(memory_space=pl.ANY),
            scratch_shapes=[pltpu.SemaphoreType.DMA((2,))]),
        compiler_params=_params(("arbitrary",)), name="moe_gather")(row_token, h_packed)


def _unpack_bf16_pairs(u):
    hi = lax.bitcast_convert_type(u & jnp.uint32(0xFFFF0000), F32).astype(BF16)
    lo = lax.bitcast_convert_type(u << 16, F32).astype(BF16)
    return jnp.concatenate([hi, lo], axis=1)


def _moe_stage_weight(te_ref, w_ref, wb_ref):
    t = pl.program_id(1)
    prev = te_ref[jnp.maximum(t - 1, 0)]

    @pl.when(jnp.logical_or(t == 0, te_ref[t] != prev))
    def _():
        wb_ref[...] = w_ref[0, 0].astype(BF16)


def _moe_swiglu_kernel(te_ref, nu_ref, x_ref, w1_ref, w3_ref, o_ref, w1b_ref, w3b_ref):
    _moe_stage_weight(te_ref, w1_ref, w1b_ref)
    _moe_stage_weight(te_ref, w3_ref, w3b_ref)
    used = pl.program_id(1) < nu_ref[0]

    @pl.when(used)
    def _():
        x = _unpack_bf16_pairs(x_ref[...])
        a = _dot(x, w1b_ref[...])
        o_ref[...] = (a * jax.nn.sigmoid(a) * _dot(x, w3b_ref[...])).astype(o_ref.dtype)

    @pl.when(jnp.logical_not(used))
    def _():
        o_ref[...] = jnp.zeros(o_ref.shape, o_ref.dtype)


def _moe_down_kernel(te_ref, nu_ref, x_ref, w_ref, o_ref, wb_ref):
    _moe_stage_weight(te_ref, w_ref, wb_ref)
    used = pl.program_id(1) < nu_ref[0]

    @pl.when(used)
    def _():
        o_ref[...] = _dot(x_ref[...], wb_ref[...])

    @pl.when(jnp.logical_not(used))
    def _():
        o_ref[...] = jnp.zeros(o_ref.shape, o_ref.dtype)


def _moe_wspec(layer, k, bn):
    return pl.BlockSpec((1, 1, k, bn), lambda j, t, te, nu: (layer, te[t], 0, j))


def moe_grouped_swiglu(xs, w1e, w3e, layer, tile_expert, n_used):
    n_rows = xs.shape[0]
    k, n = w1e.shape[2], w1e.shape[3]
    bn = _tile(n, 512, LANES)
    return pl.pallas_call(
        _moe_swiglu_kernel, out_shape=jax.ShapeDtypeStruct((n_rows, n), BF16),
        grid_spec=pltpu.PrefetchScalarGridSpec(
            num_scalar_prefetch=2, grid=(n // bn, n_rows // MOE_TILE),
            in_specs=[pl.BlockSpec((MOE_TILE, k // 2), lambda j, t, te, nu: (t, 0)),
                      _moe_wspec(layer, k, bn), _moe_wspec(layer, k, bn)],
            out_specs=pl.BlockSpec((MOE_TILE, bn), lambda j, t, te, nu: (t, j)),
            scratch_shapes=[pltpu.VMEM((k, bn), BF16), pltpu.VMEM((k, bn), BF16)]),
        compiler_params=_params(("parallel", "arbitrary")), name="moe_swiglu",
    )(tile_expert, n_used, xs, w1e, w3e)


def moe_grouped_down(hm, w2e, layer, tile_expert, n_used):
    n_rows, k = hm.shape
    n = w2e.shape[3]
    bn = _tile(n, 1024, LANES)
    return pl.pallas_call(
        _moe_down_kernel, out_shape=jax.ShapeDtypeStruct((n_rows, n), F32),
        grid_spec=pltpu.PrefetchScalarGridSpec(
            num_scalar_prefetch=2, grid=(n // bn, n_rows // MOE_TILE),
            in_specs=[pl.BlockSpec((MOE_TILE, k), lambda j, t, te, nu: (t, 0)),
                      _moe_wspec(layer, k, bn)],
            out_specs=pl.BlockSpec((MOE_TILE, bn), lambda j, t, te, nu: (t, j)),
            scratch_shapes=[pltpu.VMEM((k, bn), BF16)]),
        compiler_params=_params(("parallel", "arbitrary")), name="moe_down",
    )(tile_expert, n_used, hm, w2e)


def _moe_combine_kernel(pos_ref, y_hbm, res_ref, route_ref, gate_ref, fg_ref, o_ref, buf, sems, *,
                        blocks_per_batch, n_batch, final_norm_out):
    i = pl.program_id(0)
    n_blocks = pl.num_programs(0)
    rows = res_ref.shape[0]

    def issue_block(blk, slot):
        base = blk * rows * TOP_K

        def issue(r, carry):
            for s in range(TOP_K):
                _row_copy(y_hbm, pos_ref[base + TOP_K * r + s], buf.at[slot, s, pl.ds(r, 1)],
                          sems.at[slot]).start()
            return carry

        lax.fori_loop(0, rows, issue, 0, unroll=DMA_LOOP_UNROLL)

    @pl.when(i == 0)
    def _():
        issue_block(0, 0)

    @pl.when(i + 1 < n_blocks)
    def _():
        issue_block(i + 1, (i + 1) % 2)

    slot = i % 2

    def wait(r, carry):
        for s in range(TOP_K):
            _row_copy(y_hbm, 0, buf.at[slot, s, pl.ds(r, 1)], sems.at[slot]).wait()
        return carry

    lax.fori_loop(0, rows, wait, 0, unroll=DMA_LOOP_UNROLL)
    w = route_ref[...]
    y = w[:, 0:1] * buf[slot, 0] + w[:, 1:2] * buf[slot, 1]
    r = _mod_row(i, blocks_per_batch, n_batch)
    x_new = res_ref[...] + gate_ref[pl.ds(r, 1), :] * y
    o_ref[...] = _rms(x_new, fg_ref[...]) if final_norm_out else x_new


def moe_combine(ys, pos, res, route, mod_l, which, rows, seq_rows, n_batch, final_g=None):
    d = ys.shape[1]
    bm = _tile(math.gcd(rows, seq_rows), MOE_COMBINE_ROWS, 8)
    kern = functools.partial(_moe_combine_kernel, blocks_per_batch=seq_rows // bm,
                             n_batch=n_batch, final_norm_out=final_g is not None)
    fg = jnp.ones((1, d), F32) if final_g is None else final_g.reshape(1, d)
    return pl.pallas_call(
        kern, out_shape=jax.ShapeDtypeStruct((rows, d), F32),
        grid_spec=pltpu.PrefetchScalarGridSpec(
            num_scalar_prefetch=1, grid=(rows // bm,),
            in_specs=[pl.BlockSpec(memory_space=pl.ANY),
                      pl.BlockSpec((bm, d), lambda i, p: (i, 0)),
                      pl.BlockSpec((bm, LANES), lambda i, p: (i, 0)),
                      pl.BlockSpec((COND_ROWS, d), lambda i, p: (0, which)),
                      pl.BlockSpec((1, d), lambda i, p: (0, 0))],
            out_specs=pl.BlockSpec((bm, d), lambda i, p: (i, 0)),
            scratch_shapes=[pltpu.VMEM((2, TOP_K, bm, d), F32), pltpu.SemaphoreType.DMA((2,))]),
        compiler_params=_params(("arbitrary",)), name="moe_combine",
    )(pos, ys, res, route, mod_l, fg)


def _merge_kernel(h_ref, f_ref, c_ref, a_ref, wg0, wg1, wg2, bg0, bg1, bg2, wf, wc, wa, o_ref,
                  wg0b, wg1b, wg2b, wfb, wcb, wab):
    for w_ref, wb_ref in ((wg0, wg0b), (wg1, wg1b), (wg2, wg2b), (wf, wfb), (wc, wcb), (wa, wab)):
        _stage_weight(w_ref, wb_ref)
    h = h_ref[...]

    def branch(wg, bg, x_ref, w):
        gate = jax.nn.sigmoid(_dot(h, wg[...]) + bg[0])
        return gate * _dot(x_ref[...], w[...])

    y = (branch(wg0b, bg0, f_ref, wfb) + branch(wg1b, bg1, c_ref, wcb)
         + branch(wg2b, bg2, a_ref, wab))
    o_ref[...] = y.astype(o_ref.dtype)


def merge_branches(h, four, conv, att, w_gate, b_gate, w_four_out, w_conv_out, w_attn_out, layer,
                   rows, seq_rows):
    d = h.shape[1]
    bm = _tile(math.gcd(rows, seq_rows), 512, BF16_SUBLANES)
    bn = _tile(d, 256, LANES)
    nb = d // bn
    depth = b_gate.shape[0]
    b3 = b_gate.reshape(depth, 1, 3 * d)

    def row_spec(width):
        return pl.BlockSpec((bm, width), lambda j, i: (i, 0))

    def col_spec(kdim, blk_off):
        return pl.BlockSpec((1, kdim, bn), lambda j, i: (layer, 0, blk_off + j),
                            pipeline_mode=pl.Buffered(1))

    kdims = (d, d, d, four.shape[1], conv.shape[1], att.shape[1])
    in_specs = ([row_spec(d), row_spec(four.shape[1]), row_spec(conv.shape[1]),
                 row_spec(att.shape[1])]
                + [col_spec(d, r * nb) for r in range(3)]
                + [col_spec(1, r * nb) for r in range(3)]
                + [col_spec(kd, 0) for kd in kdims[3:]])
    return pl.pallas_call(
        _merge_kernel, out_shape=jax.ShapeDtypeStruct((rows, d), BF16), grid=(nb, rows // bm),
        in_specs=in_specs, out_specs=pl.BlockSpec((bm, bn), lambda j, i: (i, j)),
        scratch_shapes=[pltpu.VMEM((kd, bn), BF16) for kd in kdims],
        compiler_params=_params(("parallel", "arbitrary")), name="merge_branches",
    )(h, four, conv, att, w_gate, w_gate, w_gate, b3, b3, b3, w_four_out, w_conv_out, w_attn_out)


def _fourier_tables(n_pos, group_dim, n2):
    n1 = n_pos // n2
    a = np.arange(n1)
    ang1 = 2.0 * np.pi * np.outer(a, a) / n1
    w1 = np.concatenate([np.cos(ang1), -np.sin(ang1)], axis=0)
    k = (np.arange(n1)[:, None] + n1 * np.arange(n2)[None, :])[:, :, None]
    ang2 = 2.0 * np.pi * ((k * np.arange(n2)[None, None, :]) % n_pos) / n_pos
    gc, gs = np.cos(ang2), np.sin(ang2)
    g = np.concatenate([np.concatenate([gc, gs], axis=2),
                        np.concatenate([-gs, gc], axis=2)], axis=1)
    c = np.arange(group_dim)
    angc = 2.0 * np.pi * np.outer(c, c) / group_dim
    scale = 1.0 / math.sqrt(n_pos * group_dim)
    as_bf16 = lambda t: jnp.asarray(t, dtype=F32).astype(BF16)
    return as_bf16(w1), as_bf16(g), as_bf16(np.cos(angc)), as_bf16(np.sin(angc)), scale


def _four1_kernel(w_ref, x_ref, o_ref):
    o_ref[0] = _dot(w_ref[...], x_ref[...]).astype(o_ref.dtype)


def _four2_kernel(*refs, scale, group_dim, complex_in):
    if complex_in:
        g_ref, zr_ref, zi_ref, cc_ref, sc_ref, o_ref = refs
        z = jnp.concatenate([zr_ref[0], zi_ref[0]], axis=0)
        g = g_ref[0]
    else:
        g_ref, zr_ref, cc_ref, sc_ref, o_ref = refs
        z = zr_ref[...]
        g = g_ref[0][:, :z.shape[0]]
    p = _dot(g, z)
    half = p.shape[0] // 2
    pr, pi = p[:half].astype(BF16), p[half:].astype(BF16)
    cc, sc = cc_ref[...], sc_ref[...]
    for grp in range(o_ref.shape[-1] // group_dim):
        sl = slice(grp * group_dim, (grp + 1) * group_dim)
        o = _dot(pr[:, sl], cc) + _dot(pi[:, sl], sc)
        o_ref[:, sl] = (o * scale).astype(o_ref.dtype)


def fourier_latent(f, n_batch, seq):
    total_rows = n_batch * seq
    fdim = f.shape[1]
    gd = fdim // FOURIER_GROUPS
    n2 = FFT_INNER
    n1 = seq // n2
    w1, g, cc, sc, scale = _fourier_tables(seq, gd, n2)
    wide = n2 * fdim
    bn = _tile(wide, 8192, LANES)
    z = pl.pallas_call(
        _four1_kernel, out_shape=jax.ShapeDtypeStruct((n_batch, 2 * n1, wide), BF16),
        grid=(n_batch, wide // bn),
        in_specs=[pl.BlockSpec((2 * n1, n1), lambda b, j: (0, 0)),
                  pl.BlockSpec((n1, bn), lambda b, j: (b, j))],
        out_specs=pl.BlockSpec((1, 2 * n1, bn), lambda b, j: (b, 0, j)),
        compiler_params=_params(("parallel", "parallel")), name="fourier_stage1",
    )(w1, f.reshape(f.shape[0] // n2, wide))
    z = z.reshape(n_batch * 2 * n1, n2, fdim)
    kern = functools.partial(_four2_kernel, scale=scale, group_dim=gd, complex_in=True)
    out = pl.pallas_call(
        kern, out_shape=jax.ShapeDtypeStruct((total_rows // n1, n1 * fdim), BF16),
        grid=(n_batch, n1),
        in_specs=[pl.BlockSpec((1, 2 * n2, 2 * n2), lambda b, k: (k, 0, 0)),
                  pl.BlockSpec((1, n2, fdim), lambda b, k: (b * 2 * n1 + k, 0, 0)),
                  pl.BlockSpec((1, n2, fdim), lambda b, k: (b * 2 * n1 + n1 + k, 0, 0)),
                  pl.BlockSpec((gd, gd), lambda b, k: (0, 0)),
                  pl.BlockSpec((gd, gd), lambda b, k: (0, 0))],
        out_specs=pl.BlockSpec((n2, fdim), lambda b, k: (b, k)),
        compiler_params=_params(("parallel", "parallel")), name="fourier_stage2",
    )(g, z, z, cc, sc)
    return out.reshape(total_rows, fdim)


def fourier_context(f, n_batch, ctx_len, lat_rows):
    fdim = f.shape[1]
    gd = fdim // FOURIER_GROUPS
    n2 = ctx_len
    _, g, cc, sc, scale = _fourier_tables(ctx_len, gd, n2)
    first = lat_rows // ctx_len
    kern = functools.partial(_four2_kernel, scale=scale, group_dim=gd, complex_in=False)
    return pl.pallas_call(
        kern, out_shape=jax.ShapeDtypeStruct((n_batch * ctx_len, fdim), BF16), grid=(n_batch,),
        in_specs=[pl.BlockSpec((1, 2 * n2, 2 * n2), lambda b: (0, 0, 0)),
                  pl.BlockSpec((ctx_len, fdim), lambda b: (first + b, 0)),
                  pl.BlockSpec((gd, gd), lambda b: (0, 0)),
                  pl.BlockSpec((gd, gd), lambda b: (0, 0))],
        out_specs=pl.BlockSpec((ctx_len, fdim), lambda b: (b, 0)),
        compiler_params=_params(("parallel",)), name="fourier_context",
    )(g, f, cc, sc)


F32_SUBLANES = 8


def _conv_kernel(prev_ref, cur_ref, next_ref, w_ref, b_ref, g_ref, bb_ref, o_ref, z_ref, zs_ref,
                 *, starts, ends, chunk):
    i = pl.program_id(0)
    ts, c = o_ref.shape

    def glu(ref):
        v = ref[...].astype(F32)
        return v[:, :c] * jax.nn.sigmoid(v[:, c:])

    def any_eq(vals):
        hit = i == vals[0]
        for v in vals[1:]:
            hit = jnp.logical_or(hit, i == v)
        return hit

    z_ref[0:HALO_ROWS, :] = jnp.where(any_eq(starts), 0.0, glu(prev_ref))
    z_ref[HALO_ROWS:HALO_ROWS + ts, :] = glu(cur_ref)
    z_ref[HALO_ROWS + ts:, :] = jnp.where(any_eq(ends), 0.0, glu(next_ref))
    n_shift = zs_ref.shape[1]
    for b in range(1, F32_SUBLANES):
        zs_ref[b - 1] = z_ref[b:b + n_shift, :]
    bias, gamma, beta = b_ref[...], g_ref[...], bb_ref[...]
    for r0 in range(0, ts, chunk):
        acc = jnp.zeros((chunk, c), F32)
        for t in range(CONV_WIDTH):
            lo = r0 + t + HALO_ROWS - CONV_PAD
            b = lo % F32_SUBLANES
            src = z_ref if b == 0 else zs_ref.at[b - 1]
            acc = acc + w_ref[t:t + 1, :] * src[lo - b:lo - b + chunk, :]
        acc = acc + bias
        mu = jnp.mean(acc, axis=-1, keepdims=True)
        dev = acc - mu
        var = jnp.mean(dev * dev, axis=-1, keepdims=True)
        y = dev * lax.rsqrt(var + LN_EPS) * gamma + beta
        o_ref[r0:r0 + chunk, :] = (y * jax.nn.sigmoid(y)).astype(o_ref.dtype)


def conformer_conv(cab, w_dw, b_dw, ln_g, ln_b, rows, seq_lens):
    c = cab.shape[1] // 2
    ts = _tile(math.gcd(*seq_lens) if len(seq_lens) > 1 else seq_lens[0], 256, HALO_ROWS)
    per = ts // HALO_ROWS
    starts, ends, pos = [], [], 0
    for n in seq_lens:
        starts.append(pos // ts)
        pos += n
        ends.append(pos // ts - 1)
    assert pos == rows
    last_halo = rows // HALO_ROWS - 1
    kern = functools.partial(_conv_kernel, starts=tuple(starts), ends=tuple(ends), chunk=16)
    vec = lambda v: v.reshape(1, c)
    vspec = pl.BlockSpec((1, c), lambda i: (0, 0))
    return pl.pallas_call(
        kern, out_shape=jax.ShapeDtypeStruct((rows, c), BF16), grid=(rows // ts,),
        in_specs=[pl.BlockSpec((HALO_ROWS, 2 * c), lambda i: (jnp.maximum(i * per - 1, 0), 0)),
                  pl.BlockSpec((ts, 2 * c), lambda i: (i, 0)),
                  pl.BlockSpec((HALO_ROWS, 2 * c),
                               lambda i: (jnp.minimum((i + 1) * per, last_halo), 0)),
                  pl.BlockSpec((CONV_WIDTH, c), lambda i: (0, 0)), vspec, vspec, vspec],
        out_specs=pl.BlockSpec((ts, c), lambda i: (i, 0)),
        scratch_shapes=[pltpu.VMEM((ts + 2 * HALO_ROWS, c), F32),
                        pltpu.VMEM((F32_SUBLANES - 1, ts + 2 * HALO_ROWS - F32_SUBLANES, c), F32)],
        compiler_params=_params(("parallel",)), name="conformer_conv",
    )(cab, cab, cab, w_dw, vec(b_dw), vec(ln_g), vec(ln_b))


ATTN_ROW_GROUP = 64


def _lane_tile(x, reps):
    return x if reps == 1 else jnp.concatenate([x] * reps, axis=1)


def _attn_kernel(*refs, lam_init, tk, has_latent):
    if has_latent:
        (q_ref, k_ref, v_ref, kc_ref, vc_ref, lam_ref, g_ref, o_ref,
         m_ref, l_ref, a_ref, acc_ref) = refs[:12]
        slot_refs = refs[12:]
    else:
        (q_ref, kc_ref, vc_ref, lam_ref, g_ref, o_ref,
         m_ref, l_ref, a_ref, acc_ref) = refs[:10]
        slot_refs = refs[10:]
    s_ref, p_ref = slot_refs[:len(slot_refs) // 2], slot_refs[len(slot_refs) // 2:]
    hd = ATTN_HEAD_DIM
    m_ref[...] = jnp.full(m_ref.shape, -jnp.inf, F32)
    l_ref[...] = jnp.zeros(l_ref.shape, F32)
    acc_ref[...] = jnp.zeros(acc_ref.shape, F32)

    def scores(slot, k):
        n = k.shape[0]
        for mp in range(2):
            s_ref[slot][mp, :, :n] = lax.dot_general(
                q_ref[:, mp * hd:(mp + 1) * hd], k[:, mp * hd:(mp + 1) * hd],
                (((1,), (1,)), ((), ())), preferred_element_type=F32)

    def softmax_pv(slot, v):
        n = v.shape[0]
        tq = q_ref.shape[0]
        sub = min(ATTN_ROW_GROUP, tq)
        for mp in range(2):
            for r0 in range(0, tq, sub):
                rows = slice(r0, r0 + sub)
                s = s_ref[slot][mp, rows, :n]
                m_prev = m_ref[mp, rows]
                m_new = jnp.maximum(m_prev, jnp.max(s, axis=-1, keepdims=True))
                alpha = jnp.exp2(m_prev - m_new)
                p = jnp.exp2(s - _lane_tile(m_new, n // LANES))
                l_ref[mp, rows] = alpha * l_ref[mp, rows] + jnp.sum(p, axis=-1, keepdims=True)
                m_ref[mp, rows] = m_new
                a_ref[mp, rows] = alpha
                p_ref[slot][mp, rows, :n] = p.astype(BF16)
            acc_ref[mp] = (acc_ref[mp] * _lane_tile(a_ref[mp], ATTN_V_DIM // LANES)
                           + _dot(p_ref[slot][mp, :, :n], v))

    if has_latent:
        n_chunks = k_ref.shape[0] // tk
        assert n_chunks >= 2 and n_chunks % 2 == 0

        def chunk(ref, c):
            off = c * tk if isinstance(c, int) else pl.multiple_of(c * tk, tk)
            return ref[pl.ds(off, tk), :]

        scores(0, chunk(k_ref, 0))

        def body(i, carry):
            c = 2 * i
            scores(1, chunk(k_ref, c + 1))
            softmax_pv(0, chunk(v_ref, c))
            scores(0, chunk(k_ref, c + 2))
            softmax_pv(1, chunk(v_ref, c + 1))
            return carry

        lax.fori_loop(0, n_chunks // 2 - 1, body, 0)
        c = n_chunks - 2
        scores(1, chunk(k_ref, c + 1))
        softmax_pv(0, chunk(v_ref, c))
        scores(0, kc_ref[...])
        softmax_pv(1, chunk(v_ref, c + 1))
        softmax_pv(0, vc_ref[...])
    else:
        scores(0, kc_ref[...])
        softmax_pv(0, vc_ref[...])

    lam_v = lam_ref[...]
    lam = (jnp.exp(jnp.sum(lam_v[0:1] * lam_v[1:2], axis=-1, keepdims=True))
           - jnp.exp(jnp.sum(lam_v[2:3] * lam_v[3:4], axis=-1, keepdims=True)) + lam_init)
    reps = ATTN_V_DIM // LANES
    o = (acc_ref[0] * _lane_tile(1.0 / l_ref[0], reps)
         - lam * (acc_ref[1] * _lane_tile(1.0 / l_ref[1], reps)))
    o = _rms(o, g_ref[...]) * (1.0 - lam_init)
    o_ref[...] = o.astype(o_ref.dtype)


def _attn_scratch(tq, tk, slots):
    stat = pltpu.VMEM((2, tq, LANES), F32)
    return ([stat, stat, stat, pltpu.VMEM((2, tq, ATTN_V_DIM), F32)]
            + [pltpu.VMEM((2, tq, tk), F32)] * slots + [pltpu.VMEM((2, tq, tk), BF16)] * slots)


def attention_latent(qkv, lam_vecs, subln_g, lam_init, n_batch, seq, ctx_len):
    total_rows = n_batch * seq
    tq = _tile(seq, 512, BF16_SUBLANES)
    tk = _tile(seq, 1024, LANES)
    qb = seq // tq
    kblk = ATTN_QK_DIM // ATTN_V_DIM
    vblk = 2 * ATTN_QK_DIM // ATTN_V_DIM
    first_ctx = n_batch * seq // ctx_len
    kern = functools.partial(_attn_kernel, lam_init=lam_init, tk=tk, has_latent=True)
    w = ATTN_V_DIM
    return pl.pallas_call(
        kern, out_shape=jax.ShapeDtypeStruct((total_rows, ATTN_DIM), BF16),
        grid=(n_batch, ATTN_HEADS, qb),
        in_specs=[pl.BlockSpec((tq, w), lambda b, h, i: (b * qb + i, h)),
                  pl.BlockSpec((seq, w), lambda b, h, i: (b, kblk + h)),
                  pl.BlockSpec((seq, w), lambda b, h, i: (b, vblk + h)),
                  pl.BlockSpec((ctx_len, w), lambda b, h, i: (first_ctx + b, kblk + h)),
                  pl.BlockSpec((ctx_len, w), lambda b, h, i: (first_ctx + b, vblk + h)),
                  pl.BlockSpec((4, ATTN_HEAD_DIM), lambda b, h, i: (0, 0)),
                  pl.BlockSpec((1, w), lambda b, h, i: (0, 0))],
        out_specs=pl.BlockSpec((tq, w), lambda b, h, i: (b * qb + i, h)),
        scratch_shapes=_attn_scratch(tq, max(tk, ctx_len), 2),
        compiler_params=_params(("parallel", "parallel", "parallel")), name="attention_latent",
    )(qkv, qkv, qkv, qkv, qkv, lam_vecs, subln_g.reshape(1, w))


def attention_context(qkv, lam_vecs, subln_g, lam_init, n_batch, ctx_len, lat_rows):
    kblk = ATTN_QK_DIM // ATTN_V_DIM
    vblk = 2 * ATTN_QK_DIM // ATTN_V_DIM
    first = lat_rows // ctx_len
    kern = functools.partial(_attn_kernel, lam_init=lam_init, tk=ctx_len, has_latent=False)
    w = ATTN_V_DIM
    return pl.pallas_call(
        kern, out_shape=jax.ShapeDtypeStruct((n_batch * ctx_len, ATTN_DIM), BF16),
        grid=(n_batch, ATTN_HEADS),
        in_specs=[pl.BlockSpec((ctx_len, w), lambda b, h: (first + b, h)),
                  pl.BlockSpec((ctx_len, w), lambda b, h: (first + b, kblk + h)),
                  pl.BlockSpec((ctx_len, w), lambda b, h: (first + b, vblk + h)),
                  pl.BlockSpec((4, ATTN_HEAD_DIM), lambda b, h: (0, 0)),
                  pl.BlockSpec((1, w), lambda b, h: (0, 0))],
        out_specs=pl.BlockSpec((ctx_len, w), lambda b, h: (b, h)),
        scratch_shapes=_attn_scratch(ctx_len, ctx_len, 1),
        compiler_params=_params(("parallel", "parallel")), name="attention_context",
    )(qkv, qkv, qkv, lam_vecs, subln_g.reshape(1, w))


def _rope_tables(n):
    pos = np.arange(n)
    inv = ROPE_BASE ** (-np.arange(ROPE_FREQS, dtype=np.float64) / ROPE_FREQS)
    ang = [(pos // GRID_W)[:, None] * inv, (pos % GRID_W)[:, None] * inv]
    zero = np.zeros((n, ROPE_FREQS))
    cos = np.concatenate([np.cos(ang[0])] * 2 + [np.cos(ang[1])] * 2, axis=1)
    s_up = np.concatenate([zero, np.sin(ang[0]), zero, np.sin(ang[1])], axis=1)
    s_dn = np.concatenate([-np.sin(ang[0]), zero, -np.sin(ang[1]), zero], axis=1)
    return tuple(jnp.asarray(t, dtype=F32) for t in (cos, s_up, s_dn))


def kernel(x, c, ctx, c_ctx, norm1_g, w_mod, b_mod, w_in, w_gate, b_gate, w_four_out, w_dw, b_dw,
           conv_ln_g, conv_ln_b, w_conv_out, lam_q1, lam_k1, lam_q2, lam_k2, subln_g, w_attn_out,
           w_o, norm2_g, w1, w3, w2, w_router, b_router, w1e, w3e, w2e, norm_f_g):
    n_batch, seq, d = x.shape
    ctx_len = ctx.shape[1]
    depth = w_in.shape[0]
    lat_rows, ctx_rows = n_batch * seq, n_batch * ctx_len
    all_rows = lat_rows + ctx_rows
    fdim = w_four_out.shape[1]
    cdim = w_conv_out.shape[1]
    assert n_batch < COND_ROWS and ctx_len % HALO_ROWS == 0 and seq % FFT_INNER == 0

    xa = jnp.concatenate([x.reshape(lat_rows, d), ctx.reshape(ctx_rows, d)], axis=0)
    cond = jnp.zeros((COND_ROWS, d), F32).at[:n_batch].set(c).at[n_batch].set(c_ctx)
    mod = ada_params_all(cond, w_mod, b_mod)
    rope = _rope_tables(seq)

    for l in range(depth):
        last = l == depth - 1
        rows = lat_rows if last else all_rows
        seqs = [seq] * n_batch + ([] if last else [ctx_len] * n_batch)
        lam_init = 0.8 - 0.6 * math.exp(-0.3 * l)
        lam_vecs = jnp.stack([lam_q1[l], lam_k1[l], lam_q2[l], lam_k2[l]]).astype(F32)

        h = norm_modulate(xa, norm1_g[l], mod[l], 0, all_rows, seq, n_batch)
        f = mm_plain(h, w_in, l, 0, fdim, rows, seq)
        cab = mm_plain(h, w_in, l, fdim, 2 * cdim, rows, seq)
        qkv = mm_qkv_rope(h, w_in, l, fdim + 2 * cdim, all_rows, lat_rows, seq, rope)
        four = fourier_latent(f, n_batch, seq)
        att = attention_latent(qkv, lam_vecs, subln_g[l], lam_init, n_batch, seq, ctx_len)
        if not last:
            four = jnp.concatenate([four, fourier_context(f, n_batch, ctx_len, lat_rows)], axis=0)
            att = jnp.concatenate([att, attention_context(qkv, lam_vecs, subln_g[l], lam_init,
                                                          n_batch, ctx_len, lat_rows)], axis=0)
        conv = conformer_conv(cab, w_dw[l], b_dw[l], conv_ln_g[l], conv_ln_b[l], rows, seqs)
        y = merge_branches(h, four, conv, att, w_gate, b_gate, w_four_out, w_conv_out,
                           w_attn_out, l, rows, seq)
        xa = mm_residual(y, w_o, l, xa, mod[l], 2, rows, seq, n_batch)

        i = l // 2
        if l % 2 == 0:
            h2 = norm_modulate(xa, norm2_g[l], mod[l], 3, rows, seq, n_batch)
            hm = mm_swiglu(h2, w1, w3, i, rows, seq)
            xa = mm_residual(hm, w2, i, xa, mod[l], 5, rows, seq, n_batch)
        else:
            h2p, route = norm_modulate(xa, norm2_g[l], mod[l], 3, rows, seq, n_batch,
                                       router=(w_router[i], b_router[i]))
            row_token, pos, tile_expert, n_used = moe_plan(route, w1e.shape[1])
            xs = moe_gather(h2p, row_token)
            hm = moe_grouped_swiglu(xs, w1e, w3e, i, tile_expert, n_used)
            ys = moe_grouped_down(hm, w2e, i, tile_expert, n_used)
            xa = moe_combine(ys, pos, xa, route, mod[l], 5, rows, seq, n_batch,
                             final_g=norm_f_g if last else None)
            if last:
                return xa.reshape(n_batch, seq, d)

    return final_norm(xa, norm_f_g, lat_rows).reshape(n_batch, seq, d)
```

```python
import functools
import math

import numpy as np
import jax
import jax.numpy as jnp
from jax import lax
from jax.experimental import pallas as pl
from jax.experimental.pallas import tpu as pltpu

F32 = jnp.float32
BF16 = jnp.bfloat16

GRID_W = 64
EPS = 1e-6
LN_EPS = 1e-5
N_MOD = 6
FOURIER_GROUPS = 4
CONV_WIDTH = 31
CONV_PAD = CONV_WIDTH // 2
ATTN_HEADS = 8
ATTN_HEAD_DIM = 128
ATTN_V_DIM = 2 * ATTN_HEAD_DIM
ATTN_QK_DIM = ATTN_HEADS * 2 * ATTN_HEAD_DIM
ATTN_DIM = ATTN_HEADS * ATTN_V_DIM
ROPE_BASE = 10000.0
ROPE_FREQS = ATTN_HEAD_DIM // 4
N_EXPERTS = 8
TOP_K = 2

LANES = 128
BF16_SUBLANES = 16
V7X_VMEM_LIMIT_BYTES = 56 * 1024 * 1024
FFT_INNER = 128
COND_ROWS = 8
HALO_ROWS = BF16_SUBLANES


def _tile(n, pref, align):
    t = (min(pref, n) // align) * align
    while t > align and n % t:
        t -= align
    assert t >= align and n % t == 0, (n, pref, align)
    return t


def _params(semantics):
    return pltpu.CompilerParams(dimension_semantics=semantics,
                                vmem_limit_bytes=V7X_VMEM_LIMIT_BYTES)


def _dot(a, b):
    return jnp.dot(a, b, preferred_element_type=F32)


def _ada_kernel(c_ref, w_ref, b_ref, o_ref):
    c = c_ref[...]
    s = (c * jax.nn.sigmoid(c)).astype(BF16)
    o_ref[0] = _dot(s, w_ref[0].astype(BF16)) + b_ref[0]


def ada_params_all(cond, w_mod, b_mod):
    depth, d, n = w_mod.shape
    bn = _tile(n, 512, LANES)
    return pl.pallas_call(
        _ada_kernel,
        out_shape=jax.ShapeDtypeStruct((depth, COND_ROWS, n), F32),
        grid=(depth, n // bn),
        in_specs=[pl.BlockSpec((COND_ROWS, d), lambda l, j: (0, 0)),
                  pl.BlockSpec((1, d, bn), lambda l, j: (l, 0, j)),
                  pl.BlockSpec((1, 1, bn), lambda l, j: (l, 0, j))],
        out_specs=pl.BlockSpec((1, COND_ROWS, bn), lambda l, j: (l, 0, j)),
        compiler_params=_params(("parallel", "parallel")),
        name="ada_params",
    )(cond, w_mod, b_mod.reshape(depth, 1, n))


def _mod_row(i, blocks_per_batch, n_batch):
    return jnp.minimum(i // blocks_per_batch, n_batch)


def _rms(x, g):
    ms = jnp.mean(x * x, axis=-1, keepdims=True)
    return x * lax.rsqrt(ms + EPS) * g


def _row_sources(src, lat_rows, bm):
    lat_blocks = lat_rows // bm
    if isinstance(src, tuple):
        lat, ctx = src
        off = 0
    else:
        lat = ctx = src
        off = lat_blocks if src.shape[0] > lat_rows else 0
    return (lat, ctx, lambda i: jnp.minimum(i, lat_blocks - 1),
            lambda i: off + jnp.maximum(i - lat_blocks, 0), lat_blocks)


def _pick_rows(i, lat_blocks, lat_ref, ctx_ref):
    return jnp.where(i < lat_blocks, lat_ref[...], ctx_ref[...])


def _normmod_kernel(xl_ref, xc_ref, g_ref, sh_ref, sc_ref, o_ref, *, blocks_per_batch, n_batch,
                    lat_blocks):
    i = pl.program_id(0)
    r = _mod_row(i, blocks_per_batch, n_batch)
    y = _rms(_pick_rows(i, lat_blocks, xl_ref, xc_ref), g_ref[...])
    h = y * (1.0 + sc_ref[pl.ds(r, 1), :]) + sh_ref[pl.ds(r, 1), :]
    o_ref[...] = h.astype(o_ref.dtype)


def _normmod_router_kernel(xl_ref, xc_ref, g_ref, sh_ref, sc_ref, wr_ref, br_ref, o_ref,
                           route_ref, *, blocks_per_batch, n_batch, lat_blocks):
    i = pl.program_id(0)
    r = _mod_row(i, blocks_per_batch, n_batch)
    y = _rms(_pick_rows(i, lat_blocks, xl_ref, xc_ref), g_ref[...])
    h = y * (1.0 + sc_ref[pl.ds(r, 1), :]) + sh_ref[pl.ds(r, 1), :]
    bits = lax.bitcast_convert_type(h.astype(BF16).astype(F32), jnp.uint32)
    half = bits.shape[1] // 2
    o_ref[...] = bits[:, :half] | (bits[:, half:] >> 16)
    logits = jnp.dot(h, wr_ref[...], preferred_element_type=F32,
                     precision=lax.Precision.HIGHEST) + br_ref[...]
    lane = lax.broadcasted_iota(jnp.int32, logits.shape, 1)
    v1 = jnp.max(logits, axis=-1, keepdims=True)
    i1 = jnp.min(jnp.where(logits == v1, lane, LANES), axis=-1, keepdims=True)
    rest = jnp.where(lane == i1, -jnp.inf, logits)
    v2 = jnp.max(rest, axis=-1, keepdims=True)
    i2 = jnp.min(jnp.where(rest == v2, lane, LANES), axis=-1, keepdims=True)
    e = jnp.exp(v2 - v1)
    w1 = 1.0 / (1.0 + e)
    route_ref[...] = (jnp.where(lane == 0, w1, 0.0) + jnp.where(lane == 1, e * w1, 0.0)
                      + jnp.where(lane == 2, i1.astype(F32), 0.0)
                      + jnp.where(lane == 3, i2.astype(F32), 0.0))


def _final_norm_kernel(x_ref, g_ref, o_ref):
    o_ref[...] = _rms(x_ref[...], g_ref[...])


def norm_modulate(xa, g, mod_l, which, rows, seq_rows, n_batch, router=None):
    d = mod_l.shape[1] // N_MOD
    bm = _tile(math.gcd(rows, seq_rows), 256, BF16_SUBLANES)
    x_lat, x_ctx, lat_idx, ctx_idx, lat_blocks = _row_sources(xa, n_batch * seq_rows, bm)
    kw = dict(blocks_per_batch=seq_rows // bm, n_batch=n_batch, lat_blocks=lat_blocks)
    in_specs = [pl.BlockSpec((bm, d), lambda i: (lat_idx(i), 0)),
                pl.BlockSpec((bm, d), lambda i: (ctx_idx(i), 0)),
                pl.BlockSpec((1, d), lambda i: (0, 0)),
                pl.BlockSpec((COND_ROWS, d), lambda i: (0, which)),
                pl.BlockSpec((COND_ROWS, d), lambda i: (0, which + 1))]
    args = [x_lat, x_ctx, g.reshape(1, d), mod_l, mod_l]
    h_shape = jax.ShapeDtypeStruct((rows, d), BF16)
    h_spec = pl.BlockSpec((bm, d), lambda i: (i, 0))
    if router is None:
        return pl.pallas_call(
            functools.partial(_normmod_kernel, **kw), out_shape=h_shape, grid=(rows // bm,),
            in_specs=in_specs, out_specs=h_spec, compiler_params=_params(("parallel",)),
            name="norm_modulate")(*args)
    w_r, b_r = router
    ne = w_r.shape[1]
    w_pad = jnp.zeros((d, LANES), F32).at[:, :ne].set(w_r)
    b_pad = jnp.full((1, LANES), -jnp.inf, F32).at[0, :ne].set(b_r)
    return pl.pallas_call(
        functools.partial(_normmod_router_kernel, **kw),
        out_shape=(jax.ShapeDtypeStruct((rows, d // 2), jnp.uint32),
                   jax.ShapeDtypeStruct((rows, LANES), F32)), grid=(rows // bm,),
        in_specs=in_specs + [pl.BlockSpec((d, LANES), lambda i: (0, 0)),
                             pl.BlockSpec((1, LANES), lambda i: (0, 0))],
        out_specs=(pl.BlockSpec((bm, d // 2), lambda i: (i, 0)),
                   pl.BlockSpec((bm, LANES), lambda i: (i, 0))),
        compiler_params=_params(("parallel",)), name="norm_modulate_router")(*args, w_pad, b_pad)


def final_norm(xa, g, rows):
    d = xa.shape[1]
    bm = _tile(rows, 256, 8)
    return pl.pallas_call(
        _final_norm_kernel, out_shape=jax.ShapeDtypeStruct((rows, d), F32), grid=(rows // bm,),
        in_specs=[pl.BlockSpec((bm, d), lambda i: (i, 0)), pl.BlockSpec((1, d), lambda i: (0, 0))],
        out_specs=pl.BlockSpec((bm, d), lambda i: (i, 0)),
        compiler_params=_params(("parallel",)), name="final_norm")(xa, g.reshape(1, d))


def _stage_weight(w_ref, wb_ref):
    @pl.when(pl.program_id(1) == 0)
    def _():
        wb_ref[...] = w_ref[0].astype(BF16)


def _wspec(layer, k, bn, col_blk_off=0, single_buffer=False):
    return pl.BlockSpec((1, k, bn), lambda j, i: (layer, 0, col_blk_off + j),
                        pipeline_mode=pl.Buffered(1) if single_buffer else None)


def _mm_plain_kernel(x_ref, w_ref, o_ref, wb_ref):
    _stage_weight(w_ref, wb_ref)
    o_ref[...] = _dot(x_ref[...], wb_ref[...]).astype(o_ref.dtype)


def _mm_rope_kernel(x_ref, w_ref, c_ref, s1_ref, s2_ref, o_ref, wb_ref, *, rope_cols):
    j = pl.program_id(0)
    _stage_weight(w_ref, wb_ref)

    @pl.when(j < rope_cols)
    def _():
        acc = _dot(x_ref[...], wb_ref[...])
        c, s1, s2 = c_ref[...], s1_ref[...], s2_ref[...]
        for g in range(acc.shape[1] // LANES):
            t = acc[:, g * LANES:(g + 1) * LANES]
            r = (t * c + pltpu.roll(t, ROPE_FREQS, 1) * s1
                 + pltpu.roll(t, LANES - ROPE_FREQS, 1) * s2)
            o_ref[:, g * LANES:(g + 1) * LANES] = r.astype(o_ref.dtype)

    @pl.when(j >= rope_cols)
    def _():
        o_ref[...] = _dot(x_ref[...], wb_ref[...]).astype(o_ref.dtype)


def _mm_res_kernel(x_ref, w_ref, resl_ref, resc_ref, gate_ref, o_ref, wb_ref, *,
                   blocks_per_batch, n_batch, lat_blocks):
    i = pl.program_id(1)
    r = _mod_row(i, blocks_per_batch, n_batch)
    _stage_weight(w_ref, wb_ref)
    res = _pick_rows(i, lat_blocks, resl_ref, resc_ref)
    o_ref[...] = res + gate_ref[pl.ds(r, 1), :] * _dot(x_ref[...], wb_ref[...])


def _mm_swiglu_kernel(x_ref, w1_ref, w3_ref, o_ref, w1b_ref, w3b_ref):
    _stage_weight(w1_ref, w1b_ref)
    _stage_weight(w3_ref, w3b_ref)
    x = x_ref[...]
    a = _dot(x, w1b_ref[...])
    o_ref[...] = (a * jax.nn.sigmoid(a) * _dot(x, w3b_ref[...])).astype(o_ref.dtype)


def _mm_tiles(rows, seq_rows, n, bn_pref=512):
    bm = _tile(math.gcd(rows, seq_rows), 512, BF16_SUBLANES)
    bn = _tile(n, bn_pref, LANES)
    return bm, bn


def mm_plain(x, w, layer, col_off, n, rows, seq_rows):
    k = x.shape[1]
    bm, bn = _mm_tiles(rows, seq_rows, math.gcd(n, col_off) if col_off else n, 1024)
    return pl.pallas_call(
        _mm_plain_kernel, out_shape=jax.ShapeDtypeStruct((rows, n), BF16),
        grid=(n // bn, rows // bm),
        in_specs=[pl.BlockSpec((bm, k), lambda j, i: (i, 0)), _wspec(layer, k, bn, col_off // bn)],
        out_specs=pl.BlockSpec((bm, bn), lambda j, i: (i, j)),
        scratch_shapes=[pltpu.VMEM((k, bn), BF16)],
        compiler_params=_params(("parallel", "arbitrary")), name="mm_plain")(x, w)


def mm_qkv_rope(x, w, layer, col_off, rows, lat_rows, seq, tables):
    k = x.shape[1]
    n = 2 * ATTN_QK_DIM + ATTN_DIM
    bm = _tile(math.gcd(rows, seq), 512, BF16_SUBLANES)
    bn = _tile(math.gcd(ATTN_QK_DIM, col_off), 1024, LANES)
    seq_blocks, lat_blocks, q_cols = seq // bm, lat_rows // bm, ATTN_QK_DIM // bn
    scale = ATTN_HEAD_DIM ** -0.5 * math.log2(math.e)
    cos, s_up, s_dn = tables
    zeros = jnp.zeros((bm, LANES), F32)
    cos_all = jnp.concatenate([cos * scale, cos, zeros + scale, zeros + 1.0], axis=0)
    sup_all = jnp.concatenate([s_up * scale, s_up, zeros, zeros], axis=0)
    sdn_all = jnp.concatenate([s_dn * scale, s_dn, zeros, zeros], axis=0)

    def tab_block(j, i):
        is_q = j < q_cols
        lat = jnp.where(is_q, 0, seq_blocks) + i % seq_blocks
        ctx = jnp.where(is_q, 2 * seq_blocks, 2 * seq_blocks + 1)
        return jnp.where(i < lat_blocks, lat, ctx), 0

    tab_spec = pl.BlockSpec((bm, LANES), tab_block)
    kern = functools.partial(_mm_rope_kernel, rope_cols=2 * q_cols)
    return pl.pallas_call(
        kern, out_shape=jax.ShapeDtypeStruct((rows, n), BF16), grid=(n // bn, rows // bm),
        in_specs=[pl.BlockSpec((bm, k), lambda j, i: (i, 0)),
                  _wspec(layer, k, bn, col_off // bn, single_buffer=True),
                  tab_spec, tab_spec, tab_spec],
        out_specs=pl.BlockSpec((bm, bn), lambda j, i: (i, j)),
        scratch_shapes=[pltpu.VMEM((k, bn), BF16)],
        compiler_params=_params(("parallel", "arbitrary")), name="mm_qkv_rope",
    )(x, w, cos_all, sup_all, sdn_all)


def mm_residual(x, w, layer, res, mod_l, which, rows, seq_rows, n_batch):
    _, k, n = w.shape
    bm, bn = _mm_tiles(rows, seq_rows, n, 1024)
    wide_bytes = k * bn * (4 + 2) + 2 * bm * k * 2 + 6 * bm * bn * 4
    if wide_bytes <= V7X_VMEM_LIMIT_BYTES * 7 // 8:
        wspec = _wspec(layer, k, bn, single_buffer=True)
    else:
        bm, bn = _mm_tiles(rows, seq_rows, n)
        wspec = _wspec(layer, k, bn)
    gate_blk = which * (n // bn)
    res_lat, res_ctx, lat_idx, ctx_idx, lat_blocks = _row_sources(res, n_batch * seq_rows, bm)
    kern = functools.partial(_mm_res_kernel, blocks_per_batch=seq_rows // bm, n_batch=n_batch,
                             lat_blocks=lat_blocks)
    return pl.pallas_call(
        kern, out_shape=jax.ShapeDtypeStruct((rows, n), F32), grid=(n // bn, rows // bm),
        in_specs=[pl.BlockSpec((bm, k), lambda j, i: (i, 0)), wspec,
                  pl.BlockSpec((bm, bn), lambda j, i: (lat_idx(i), j)),
                  pl.BlockSpec((bm, bn), lambda j, i: (ctx_idx(i), j)),
                  pl.BlockSpec((COND_ROWS, bn), lambda j, i: (0, gate_blk + j))],
        out_specs=pl.BlockSpec((bm, bn), lambda j, i: (i, j)),
        scratch_shapes=[pltpu.VMEM((k, bn), BF16)],
        compiler_params=_params(("parallel", "arbitrary")), name="mm_residual",
    )(x, w, res_lat, res_ctx, mod_l)


def mm_swiglu(x, w1, w3, layer, rows, seq_rows):
    _, k, n = w1.shape
    bm, bn = _mm_tiles(rows, seq_rows, n)
    return pl.pallas_call(
        _mm_swiglu_kernel, out_shape=jax.ShapeDtypeStruct((rows, n), BF16),
        grid=(n // bn, rows // bm),
        in_specs=[pl.BlockSpec((bm, k), lambda j, i: (i, 0)), _wspec(layer, k, bn),
                  _wspec(layer, k, bn)],
        out_specs=pl.BlockSpec((bm, bn), lambda j, i: (i, j)),
        scratch_shapes=[pltpu.VMEM((k, bn), BF16), pltpu.VMEM((k, bn), BF16)],
        compiler_params=_params(("parallel", "arbitrary")), name="mm_swiglu")(x, w1, w3)


MOE_TILE = 512
MOE_GATHER_ROWS = 256
MOE_COMBINE_ROWS = 128
DMA_LOOP_UNROLL = 8


def moe_plan(route, n_experts):
    n_tok = route.shape[0]
    n_pairs = TOP_K * n_tok
    e_pair = route[:, 2:2 + TOP_K].astype(jnp.int32).reshape(n_pairs)
    onehot = (e_pair[:, None] == jnp.arange(n_experts, dtype=jnp.int32)[None, :])
    csum = jnp.cumsum(onehot.astype(jnp.int32), axis=0)
    rank = jnp.take_along_axis(csum, e_pair[:, None], axis=1)[:, 0] - 1
    counts = csum[-1]
    tiles_per = (counts + MOE_TILE - 1) // MOE_TILE
    tile_end = jnp.cumsum(tiles_per)
    pos = (tile_end - tiles_per)[e_pair] * MOE_TILE + rank
    n_rows = n_pairs + n_experts * MOE_TILE
    n_tiles = n_rows // MOE_TILE
    row_token = jnp.zeros((n_rows,), jnp.int32).at[pos].set(
        jnp.arange(n_pairs, dtype=jnp.int32) // TOP_K)
    tile_expert = jnp.minimum(
        jnp.searchsorted(tile_end, jnp.arange(n_tiles, dtype=jnp.int32), side="right"),
        n_experts - 1).astype(jnp.int32)
    return row_token, pos.astype(jnp.int32), tile_expert, tile_end[-1:].astype(jnp.int32)


def _row_copy(src_hbm, row, dst, sem):
    return pltpu.make_async_copy(src_hbm.at[pl.ds(row, 1)], dst, sem)


def _moe_gather_kernel(tok_ref, h_hbm, o_ref, buf, sems):
    i = pl.program_id(0)
    n_blocks = pl.num_programs(0)
    rows = buf.shape[1]

    def issue_block(blk, slot):
        base = blk * rows

        def issue(r, carry):
            _row_copy(h_hbm, tok_ref[base + r], buf.at[slot, pl.ds(r, 1)], sems.at[slot]).start()
            return carry

        lax.fori_loop(0, rows, issue, 0, unroll=DMA_LOOP_UNROLL)

    @pl.when(i == 0)
    def _():
        issue_block(0, 0)

    @pl.when(i + 1 < n_blocks)
    def _():
        issue_block(i + 1, (i + 1) % 2)

    slot = i % 2

    def wait(r, carry):
        _row_copy(h_hbm, 0, buf.at[slot, pl.ds(r, 1)], sems.at[slot]).wait()
        return carry

    lax.fori_loop(0, rows, wait, 0, unroll=DMA_LOOP_UNROLL)
    u = buf[slot]
    half = u.shape[1]
    o_ref[:, :half] = lax.bitcast_convert_type(u & jnp.uint32(0xFFFF0000), F32).astype(BF16)
    o_ref[:, half:] = lax.bitcast_convert_type(u << 16, F32).astype(BF16)


def moe_gather(h_packed, row_token):
    n_rows = row_token.shape[0]
    half = h_packed.shape[1]
    rows = MOE_GATHER_ROWS
    return pl.pallas_call(
        _moe_gather_kernel, out_shape=jax.ShapeDtypeStruct((n_rows, 2 * half), BF16),
        grid_spec=pltpu.PrefetchScalarGridSpec(
            num_scalar_prefetch=1, grid=(n_rows // rows,),
            in_specs=[pl.BlockSpec(memory_space=pl.ANY)],
            out_specs=pl.BlockSpec((rows, 2 * half), lambda i, tok: (i, 0)),
            scratch_shapes=[pltpu.VMEM((2, rows, half), jnp.uint32),
                            pltpu.SemaphoreType.DMA((2,))]),
        compiler_params=_params(("arbitrary",)), name="moe_gather")(row_token, h_packed)


def _moe_stage_weight(te_ref, w_ref, wb_ref):
    t = pl.program_id(1)
    prev = te_ref[jnp.maximum(t - 1, 0)]

    @pl.when(jnp.logical_or(t == 0, te_ref[t] != prev))
    def _():
        wb_ref[...] = w_ref[0, 0].astype(BF16)


def _moe_swiglu_kernel(te_ref, nu_ref, x_ref, w1_ref, w3_ref, o_ref, w1b_ref, w3b_ref):
    _moe_stage_weight(te_ref, w1_ref, w1b_ref)
    _moe_stage_weight(te_ref, w3_ref, w3b_ref)
    used = pl.program_id(1) < nu_ref[0]

    @pl.when(used)
    def _():
        x = x_ref[...]
        a = _dot(x, w1b_ref[...])
        o_ref[...] = (a * jax.nn.sigmoid(a) * _dot(x, w3b_ref[...])).astype(o_ref.dtype)

    @pl.when(jnp.logical_not(used))
    def _():
        o_ref[...] = jnp.zeros(o_ref.shape, o_ref.dtype)


def _moe_down_kernel(te_ref, nu_ref, x_ref, w_ref, o_ref, wb_ref):
    _moe_stage_weight(te_ref, w_ref, wb_ref)
    used = pl.program_id(1) < nu_ref[0]

    @pl.when(used)
    def _():
        o_ref[...] = _dot(x_ref[...], wb_ref[...])

    @pl.when(jnp.logical_not(used))
    def _():
        o_ref[...] = jnp.zeros(o_ref.shape, o_ref.dtype)


def _moe_wspec(layer, k, bn):
    return pl.BlockSpec((1, 1, k, bn), lambda j, t, te, nu: (layer, te[t], 0, j))


def moe_grouped_swiglu(xs, w1e, w3e, layer, tile_expert, n_used):
    n_rows, k = xs.shape
    n = w1e.shape[3]
    bn = _tile(n, 512, LANES)
    return pl.pallas_call(
        _moe_swiglu_kernel, out_shape=jax.ShapeDtypeStruct((n_rows, n), BF16),
        grid_spec=pltpu.PrefetchScalarGridSpec(
            num_scalar_prefetch=2, grid=(n // bn, n_rows // MOE_TILE),
            in_specs=[pl.BlockSpec((MOE_TILE, k), lambda j, t, te, nu: (t, 0)),
                      _moe_wspec(layer, k, bn), _moe_wspec(layer, k, bn)],
            out_specs=pl.BlockSpec((MOE_TILE, bn), lambda j, t, te, nu: (t, j)),
            scratch_shapes=[pltpu.VMEM((k, bn), BF16), pltpu.VMEM((k, bn), BF16)]),
        compiler_params=_params(("parallel", "arbitrary")), name="moe_swiglu",
    )(tile_expert, n_used, xs, w1e, w3e)


def moe_grouped_down(hm, w2e, layer, tile_expert, n_used):
    n_rows, k = hm.shape
    n = w2e.shape[3]
    bn = _tile(n, 1024, LANES)
    return pl.pallas_call(
        _moe_down_kernel, out_shape=jax.ShapeDtypeStruct((n_rows, n), F32),
        grid_spec=pltpu.PrefetchScalarGridSpec(
            num_scalar_prefetch=2, grid=(n // bn, n_rows // MOE_TILE),
            in_specs=[pl.BlockSpec((MOE_TILE, k), lambda j, t, te, nu: (t, 0)),
                      _moe_wspec(layer, k, bn)],
            out_specs=pl.BlockSpec((MOE_TILE, bn), lambda j, t, te, nu: (t, j)),
            scratch_shapes=[pltpu.VMEM((k, bn), BF16)]),
        compiler_params=_params(("parallel", "arbitrary")), name="moe_down",
    )(tile_expert, n_used, hm, w2e)


def _moe_combine_kernel(pos_ref, y_hbm, res_ref, route_ref, gate_ref, fg_ref, o_ref, buf, sems, *,
                        blocks_per_batch, n_batch, final_norm_out):
    i = pl.program_id(0)
    n_blocks = pl.num_programs(0)
    rows = res_ref.shape[0]

    def issue_block(blk, slot):
        base = blk * rows * TOP_K

        def issue(r, carry):
            for s in range(TOP_K):
                _row_copy(y_hbm, pos_ref[base + TOP_K * r + s], buf.at[slot, s, pl.ds(r, 1)],
                          sems.at[slot]).start()
            return carry

        lax.fori_loop(0, rows, issue, 0, unroll=DMA_LOOP_UNROLL)

    @pl.when(i == 0)
    def _():
        issue_block(0, 0)

    @pl.when(i + 1 < n_blocks)
    def _():
        issue_block(i + 1, (i + 1) % 2)

    slot = i % 2

    def wait(r, carry):
        for s in range(TOP_K):
            _row_copy(y_hbm, 0, buf.at[slot, s, pl.ds(r, 1)], sems.at[slot]).wait()
        return carry

    lax.fori_loop(0, rows, wait, 0, unroll=DMA_LOOP_UNROLL)
    w = route_ref[...]
    y = w[:, 0:1] * buf[slot, 0] + w[:, 1:2] * buf[slot, 1]
    r = _mod_row(i, blocks_per_batch, n_batch)
    x_new = res_ref[...] + gate_ref[pl.ds(r, 1), :] * y
    o_ref[...] = _rms(x_new, fg_ref[...]) if final_norm_out else x_new


def moe_combine(ys, pos, res, route, mod_l, which, rows, seq_rows, n_batch, final_g=None):
    d = ys.shape[1]
    bm = _tile(math.gcd(rows, seq_rows), MOE_COMBINE_ROWS, 8)
    kern = functools.partial(_moe_combine_kernel, blocks_per_batch=seq_rows // bm,
                             n_batch=n_batch, final_norm_out=final_g is not None)
    fg = jnp.ones((1, d), F32) if final_g is None else final_g.reshape(1, d)
    return pl.pallas_call(
        kern, out_shape=jax.ShapeDtypeStruct((rows, d), F32),
        grid_spec=pltpu.PrefetchScalarGridSpec(
            num_scalar_prefetch=1, grid=(rows // bm,),
            in_specs=[pl.BlockSpec(memory_space=pl.ANY),
                      pl.BlockSpec((bm, d), lambda i, p: (i, 0)),
                      pl.BlockSpec((bm, LANES), lambda i, p: (i, 0)),
                      pl.BlockSpec((COND_ROWS, d), lambda i, p: (0, which)),
                      pl.BlockSpec((1, d), lambda i, p: (0, 0))],
            out_specs=pl.BlockSpec((bm, d), lambda i, p: (i, 0)),
            scratch_shapes=[pltpu.VMEM((2, TOP_K, bm, d), F32), pltpu.SemaphoreType.DMA((2,))]),
        compiler_params=_params(("arbitrary",)), name="moe_combine",
    )(pos, ys, res, route, mod_l, fg)


def _merge_kernel(h_ref, f_ref, c_ref, a_ref, wg0, wg1, wg2, bg0, bg1, bg2, wf, wc, wa, o_ref,
                  wg0b, wg1b, wg2b, wfb, wcb, wab):
    for w_ref, wb_ref in ((wg0, wg0b), (wg1, wg1b), (wg2, wg2b), (wf, wfb), (wc, wcb), (wa, wab)):
        _stage_weight(w_ref, wb_ref)
    h = h_ref[...]

    def branch(wg, bg, x_ref, w):
        gate = jax.nn.sigmoid(_dot(h, wg[...]) + bg[0])
        return gate * _dot(x_ref[...], w[...])

    y = (branch(wg0b, bg0, f_ref, wfb) + branch(wg1b, bg1, c_ref, wcb)
         + branch(wg2b, bg2, a_ref, wab))
    o_ref[...] = y.astype(o_ref.dtype)


def merge_branches(h, four, conv, att, w_gate, b_gate, w_four_out, w_conv_out, w_attn_out, layer,
                   rows, seq_rows):
    d = h.shape[1]
    bm = _tile(math.gcd(rows, seq_rows), 512, BF16_SUBLANES)
    bn = _tile(d, 256, LANES)
    nb = d // bn
    depth = b_gate.shape[0]
    b3 = b_gate.reshape(depth, 1, 3 * d)

    def row_spec(width):
        return pl.BlockSpec((bm, width), lambda j, i: (i, 0))

    def col_spec(kdim, blk_off):
        return pl.BlockSpec((1, kdim, bn), lambda j, i: (layer, 0, blk_off + j),
                            pipeline_mode=pl.Buffered(1))

    kdims = (d, d, d, four.shape[1], conv.shape[1], att.shape[1])
    in_specs = ([row_spec(d), row_spec(four.shape[1]), row_spec(conv.shape[1]),
                 row_spec(att.shape[1])]
                + [col_spec(d, r * nb) for r in range(3)]
                + [col_spec(1, r * nb) for r in range(3)]
                + [col_spec(kd, 0) for kd in kdims[3:]])
    return pl.pallas_call(
        _merge_kernel, out_shape=jax.ShapeDtypeStruct((rows, d), BF16), grid=(nb, rows // bm),
        in_specs=in_specs, out_specs=pl.BlockSpec((bm, bn), lambda j, i: (i, j)),
        scratch_shapes=[pltpu.VMEM((kd, bn), BF16) for kd in kdims],
        compiler_params=_params(("parallel", "arbitrary")), name="merge_branches",
    )(h, four, conv, att, w_gate, w_gate, w_gate, b3, b3, b3, w_four_out, w_conv_out, w_attn_out)


def _fourier_tables(n_pos, group_dim, n2):
    n1 = n_pos // n2
    a = np.arange(n1)
    ang1 = 2.0 * np.pi * np.outer(a, a) / n1
    w1 = np.concatenate([np.cos(ang1), -np.sin(ang1)], axis=0)
    k = (np.arange(n1)[:, None] + n1 * np.arange(n2)[None, :])[:, :, None]
    ang2 = 2.0 * np.pi * ((k * np.arange(n2)[None, None, :]) % n_pos) / n_pos
    gc, gs = np.cos(ang2), np.sin(ang2)
    g = np.concatenate([np.concatenate([gc, gs], axis=2),
                        np.concatenate([-gs, gc], axis=2)], axis=1)
    c = np.arange(group_dim)
    angc = 2.0 * np.pi * np.outer(c, c) / group_dim
    scale = 1.0 / math.sqrt(n_pos * group_dim)
    as_bf16 = lambda t: jnp.asarray(t, dtype=F32).astype(BF16)
    return as_bf16(w1), as_bf16(g), as_bf16(np.cos(angc)), as_bf16(np.sin(angc)), scale


def _four1_kernel(w_ref, x_ref, o_ref):
    o_ref[0] = _dot(w_ref[...], x_ref[...]).astype(o_ref.dtype)


def _four2_kernel(*refs, scale, group_dim, complex_in):
    if complex_in:
        g_ref, zr_ref, zi_ref, cc_ref, sc_ref, o_ref = refs
        z = jnp.concatenate([zr_ref[0], zi_ref[0]], axis=0)
        g = g_ref[0]
    else:
        g_ref, zr_ref, cc_ref, sc_ref, o_ref = refs
        z = zr_ref[...]
        g = g_ref[0][:, :z.shape[0]]
    p = _dot(g, z)
    half = p.shape[0] // 2
    pr, pi = p[:half].astype(BF16), p[half:].astype(BF16)
    cc, sc = cc_ref[...], sc_ref[...]
    for grp in range(o_ref.shape[-1] // group_dim):
        sl = slice(grp * group_dim, (grp + 1) * group_dim)
        o = _dot(pr[:, sl], cc) + _dot(pi[:, sl], sc)
        o_ref[:, sl] = (o * scale).astype(o_ref.dtype)


def fourier_latent(f, n_batch, seq):
    total_rows = n_batch * seq
    fdim = f.shape[1]
    gd = fdim // FOURIER_GROUPS
    n2 = FFT_INNER
    n1 = seq // n2
    w1, g, cc, sc, scale = _fourier_tables(seq, gd, n2)
    wide = n2 * fdim
    bn = _tile(wide, 8192, LANES)
    z = pl.pallas_call(
        _four1_kernel, out_shape=jax.ShapeDtypeStruct((n_batch, 2 * n1, wide), BF16),
        grid=(n_batch, wide // bn),
        in_specs=[pl.BlockSpec((2 * n1, n1), lambda b, j: (0, 0)),
                  pl.BlockSpec((n1, bn), lambda b, j: (b, j))],
        out_specs=pl.BlockSpec((1, 2 * n1, bn), lambda b, j: (b, 0, j)),
        compiler_params=_params(("parallel", "parallel")), name="fourier_stage1",
    )(w1, f.reshape(f.shape[0] // n2, wide))
    z = z.reshape(n_batch * 2 * n1, n2, fdim)
    kern = functools.partial(_four2_kernel, scale=scale, group_dim=gd, complex_in=True)
    out = pl.pallas_call(
        kern, out_shape=jax.ShapeDtypeStruct((total_rows // n1, n1 * fdim), BF16),
        grid=(n_batch, n1),
        in_specs=[pl.BlockSpec((1, 2 * n2, 2 * n2), lambda b, k: (k, 0, 0)),
                  pl.BlockSpec((1, n2, fdim), lambda b, k: (b * 2 * n1 + k, 0, 0)),
                  pl.BlockSpec((1, n2, fdim), lambda b, k: (b * 2 * n1 + n1 + k, 0, 0)),
                  pl.BlockSpec((gd, gd), lambda b, k: (0, 0)),
                  pl.BlockSpec((gd, gd), lambda b, k: (0, 0))],
        out_specs=pl.BlockSpec((n2, fdim), lambda b, k: (b, k)),
        compiler_params=_params(("parallel", "parallel")), name="fourier_stage2",
    )(g, z, z, cc, sc)
    return out.reshape(total_rows, fdim)


def fourier_context(f, n_batch, ctx_len, lat_rows):
    fdim = f.shape[1]
    gd = fdim // FOURIER_GROUPS
    n2 = ctx_len
    _, g, cc, sc, scale = _fourier_tables(ctx_len, gd, n2)
    first = lat_rows // ctx_len
    kern = functools.partial(_four2_kernel, scale=scale, group_dim=gd, complex_in=False)
    return pl.pallas_call(
        kern, out_shape=jax.ShapeDtypeStruct((n_batch * ctx_len, fdim), BF16), grid=(n_batch,),
        in_specs=[pl.BlockSpec((1, 2 * n2, 2 * n2), lambda b: (0, 0, 0)),
                  pl.BlockSpec((ctx_len, fdim), lambda b: (first + b, 0)),
                  pl.BlockSpec((gd, gd), lambda b: (0, 0)),
                  pl.BlockSpec((gd, gd), lambda b: (0, 0))],
        out_specs=pl.BlockSpec((ctx_len, fdim), lambda b: (b, 0)),
        compiler_params=_params(("parallel",)), name="fourier_context",
    )(g, f, cc, sc)


F32_SUBLANES = 8


def _conv_kernel(prev_ref, cur_ref, next_ref, w_ref, b_ref, g_ref, bb_ref, o_ref, z_ref, zs_ref,
                 *, starts, ends, chunk):
    i = pl.program_id(0)
    ts, c = o_ref.shape

    def glu(ref):
        v = ref[...].astype(F32)
        return v[:, :c] * jax.nn.sigmoid(v[:, c:])

    def any_eq(vals):
        hit = i == vals[0]
        for v in vals[1:]:
            hit = jnp.logical_or(hit, i == v)
        return hit

    z_ref[0:HALO_ROWS, :] = jnp.where(any_eq(starts), 0.0, glu(prev_ref))
    z_ref[HALO_ROWS:HALO_ROWS + ts, :] = glu(cur_ref)
    z_ref[HALO_ROWS + ts:, :] = jnp.where(any_eq(ends), 0.0, glu(next_ref))
    n_shift = zs_ref.shape[1]
    for b in range(1, F32_SUBLANES):
        zs_ref[b - 1] = z_ref[b:b + n_shift, :]
    bias, gamma, beta = b_ref[...], g_ref[...], bb_ref[...]
    for r0 in range(0, ts, chunk):
        acc = jnp.zeros((chunk, c), F32)
        for t in range(CONV_WIDTH):
            lo = r0 + t + HALO_ROWS - CONV_PAD
            b = lo % F32_SUBLANES
            src = z_ref if b == 0 else zs_ref.at[b - 1]
            acc = acc + w_ref[t:t + 1, :] * src[lo - b:lo - b + chunk, :]
        acc = acc + bias
        mu = jnp.mean(acc, axis=-1, keepdims=True)
        dev = acc - mu
        var = jnp.mean(dev * dev, axis=-1, keepdims=True)
        y = dev * lax.rsqrt(var + LN_EPS) * gamma + beta
        o_ref[r0:r0 + chunk, :] = (y * jax.nn.sigmoid(y)).astype(o_ref.dtype)


def conformer_conv(cab, w_dw, b_dw, ln_g, ln_b, rows, seq_lens):
    c = cab.shape[1] // 2
    ts = _tile(math.gcd(*seq_lens) if len(seq_lens) > 1 else seq_lens[0], 256, HALO_ROWS)
    per = ts // HALO_ROWS
    starts, ends, pos = [], [], 0
    for n in seq_lens:
        starts.append(pos // ts)
        pos += n
        ends.append(pos // ts - 1)
    assert pos == rows
    last_halo = rows // HALO_ROWS - 1
    kern = functools.partial(_conv_kernel, starts=tuple(starts), ends=tuple(ends), chunk=16)
    vec = lambda v: v.reshape(1, c)
    vspec = pl.BlockSpec((1, c), lambda i: (0, 0))
    return pl.pallas_call(
        kern, out_shape=jax.ShapeDtypeStruct((rows, c), BF16), grid=(rows // ts,),
        in_specs=[pl.BlockSpec((HALO_ROWS, 2 * c), lambda i: (jnp.maximum(i * per - 1, 0), 0)),
                  pl.BlockSpec((ts, 2 * c), lambda i: (i, 0)),
                  pl.BlockSpec((HALO_ROWS, 2 * c),
                               lambda i: (jnp.minimum((i + 1) * per, last_halo), 0)),
                  pl.BlockSpec((CONV_WIDTH, c), lambda i: (0, 0)), vspec, vspec, vspec],
        out_specs=pl.BlockSpec((ts, c), lambda i: (i, 0)),
        scratch_shapes=[pltpu.VMEM((ts + 2 * HALO_ROWS, c), F32),
                        pltpu.VMEM((F32_SUBLANES - 1, ts + 2 * HALO_ROWS - F32_SUBLANES, c), F32)],
        compiler_params=_params(("parallel",)), name="conformer_conv",
    )(cab, cab, cab, w_dw, vec(b_dw), vec(ln_g), vec(ln_b))


ATTN_ROW_GROUP = 64


def _lane_tile(x, reps):
    return x if reps == 1 else jnp.concatenate([x] * reps, axis=1)


def _attn_kernel(*refs, lam_init, tk, has_latent):
    if has_latent:
        (q_ref, k_ref, v_ref, kc_ref, vc_ref, lam_ref, g_ref, o_ref,
         m_ref, l_ref, a_ref, acc_ref) = refs[:12]
        slot_refs = refs[12:]
    else:
        (q_ref, kc_ref, vc_ref, lam_ref, g_ref, o_ref,
         m_ref, l_ref, a_ref, acc_ref) = refs[:10]
        slot_refs = refs[10:]
    s_ref, p_ref = slot_refs[:len(slot_refs) // 2], slot_refs[len(slot_refs) // 2:]
    hd = ATTN_HEAD_DIM
    m_ref[...] = jnp.full(m_ref.shape, -jnp.inf, F32)
    l_ref[...] = jnp.zeros(l_ref.shape, F32)
    acc_ref[...] = jnp.zeros(acc_ref.shape, F32)

    def scores(slot, k):
        n = k.shape[0]
        for mp in range(2):
            s_ref[slot][mp, :, :n] = lax.dot_general(
                q_ref[:, mp * hd:(mp + 1) * hd], k[:, mp * hd:(mp + 1) * hd],
                (((1,), (1,)), ((), ())), preferred_element_type=F32)

    def softmax_pv(slot, v):
        n = v.shape[0]
        tq = q_ref.shape[0]
        sub = min(ATTN_ROW_GROUP, tq)
        for mp in range(2):
            for r0 in range(0, tq, sub):
                rows = slice(r0, r0 + sub)
                s = s_ref[slot][mp, rows, :n]
                m_prev = m_ref[mp, rows]
                m_new = jnp.maximum(m_prev, jnp.max(s, axis=-1, keepdims=True))
                alpha = jnp.exp2(m_prev - m_new)
                p = jnp.exp2(s - _lane_tile(m_new, n // LANES))
                l_ref[mp, rows] = alpha * l_ref[mp, rows] + jnp.sum(p, axis=-1, keepdims=True)
                m_ref[mp, rows] = m_new
                a_ref[mp, rows] = alpha
                p_ref[slot][mp, rows, :n] = p.astype(BF16)
            acc_ref[mp] = (acc_ref[mp] * _lane_tile(a_ref[mp], ATTN_V_DIM // LANES)
                           + _dot(p_ref[slot][mp, :, :n], v))

    if has_latent:
        n_chunks = k_ref.shape[0] // tk
        assert n_chunks >= 2 and n_chunks % 2 == 0

        def chunk(ref, c):
            off = c * tk if isinstance(c, int) else pl.multiple_of(c * tk, tk)
            return ref[pl.ds(off, tk), :]

        scores(0, chunk(k_ref, 0))

        def body(i, carry):
            c = 2 * i
            scores(1, chunk(k_ref, c + 1))
            softmax_pv(0, chunk(v_ref, c))
            scores(0, chunk(k_ref, c + 2))
            softmax_pv(1, chunk(v_ref, c + 1))
            return carry

        lax.fori_loop(0, n_chunks // 2 - 1, body, 0)
        c = n_chunks - 2
        scores(1, chunk(k_ref, c + 1))
        softmax_pv(0, chunk(v_ref, c))
        scores(0, kc_ref[...])
        softmax_pv(1, chunk(v_ref, c + 1))
        softmax_pv(0, vc_ref[...])
    else:
        scores(0, kc_ref[...])
        softmax_pv(0, vc_ref[...])

    lam_v = lam_ref[...]
    lam = (jnp.exp(jnp.sum(lam_v[0:1] * lam_v[1:2], axis=-1, keepdims=True))
           - jnp.exp(jnp.sum(lam_v[2:3] * lam_v[3:4], axis=-1, keepdims=True)) + lam_init)
    reps = ATTN_V_DIM // LANES
    o = (acc_ref[0] * _lane_tile(1.0 / l_ref[0], reps)
         - lam * (acc_ref[1] * _lane_tile(1.0 / l_ref[1], reps)))
    o = _rms(o, g_ref[...]) * (1.0 - lam_init)
    o_ref[...] = o.astype(o_ref.dtype)


def _attn_scratch(tq, tk, slots):
    stat = pltpu.VMEM((2, tq, LANES), F32)
    return ([stat, stat, stat, pltpu.VMEM((2, tq, ATTN_V_DIM), F32)]
            + [pltpu.VMEM((2, tq, tk), F32)] * slots + [pltpu.VMEM((2, tq, tk), BF16)] * slots)


def attention_latent(qkv, lam_vecs, subln_g, lam_init, n_batch, seq, ctx_len):
    total_rows = n_batch * seq
    tq = _tile(seq, 512, BF16_SUBLANES)
    tk = _tile(seq, 1024, LANES)
    qb = seq // tq
    kblk = ATTN_QK_DIM // ATTN_V_DIM
    vblk = 2 * ATTN_QK_DIM // ATTN_V_DIM
    first_ctx = n_batch * seq // ctx_len
    kern = functools.partial(_attn_kernel, lam_init=lam_init, tk=tk, has_latent=True)
    w = ATTN_V_DIM
    return pl.pallas_call(
        kern, out_shape=jax.ShapeDtypeStruct((total_rows, ATTN_DIM), BF16),
        grid=(n_batch, ATTN_HEADS, qb),
        in_specs=[pl.BlockSpec((tq, w), lambda b, h, i: (b * qb + i, h)),
                  pl.BlockSpec((seq, w), lambda b, h, i: (b, kblk + h)),
                  pl.BlockSpec((seq, w), lambda b, h, i: (b, vblk + h)),
                  pl.BlockSpec((ctx_len, w), lambda b, h, i: (first_ctx + b, kblk + h)),
                  pl.BlockSpec((ctx_len, w), lambda b, h, i: (first_ctx + b, vblk + h)),
                  pl.BlockSpec((4, ATTN_HEAD_DIM), lambda b, h, i: (0, 0)),
                  pl.BlockSpec((1, w), lambda b, h, i: (0, 0))],
        out_specs=pl.BlockSpec((tq, w), lambda b, h, i: (b * qb + i, h)),
        scratch_shapes=_attn_scratch(tq, max(tk, ctx_len), 2),
        compiler_params=_params(("parallel", "parallel", "parallel")), name="attention_latent",
    )(qkv, qkv, qkv, qkv, qkv, lam_vecs, subln_g.reshape(1, w))


def attention_context(qkv, lam_vecs, subln_g, lam_init, n_batch, ctx_len, lat_rows):
    kblk = ATTN_QK_DIM // ATTN_V_DIM
    vblk = 2 * ATTN_QK_DIM // ATTN_V_DIM
    first = lat_rows // ctx_len
    kern = functools.partial(_attn_kernel, lam_init=lam_init, tk=ctx_len, has_latent=False)
    w = ATTN_V_DIM
    return pl.pallas_call(
        kern, out_shape=jax.ShapeDtypeStruct((n_batch * ctx_len, ATTN_DIM), BF16),
        grid=(n_batch, ATTN_HEADS),
        in_specs=[pl.BlockSpec((ctx_len, w), lambda b, h: (first + b, h)),
                  pl.BlockSpec((ctx_len, w), lambda b, h: (first + b, kblk + h)),
                  pl.BlockSpec((ctx_len, w), lambda b, h: (first + b, vblk + h)),
                  pl.BlockSpec((4, ATTN_HEAD_DIM), lambda b, h: (0, 0)),
                  pl.BlockSpec((1, w), lambda b, h: (0, 0))],
        out_specs=pl.BlockSpec((ctx_len, w), lambda b, h: (b, h)),
        scratch_shapes=_attn_scratch(ctx_len, ctx_len, 1),
        compiler_params=_params(("parallel", "parallel")), name="attention_context",
    )(qkv, qkv, qkv, lam_vecs, subln_g.reshape(1, w))


def _rope_tables(n):
    pos = np.arange(n)
    inv = ROPE_BASE ** (-np.arange(ROPE_FREQS, dtype=np.float64) / ROPE_FREQS)
    ang = [(pos // GRID_W)[:, None] * inv, (pos % GRID_W)[:, None] * inv]
    zero = np.zeros((n, ROPE_FREQS))
    cos = np.concatenate([np.cos(ang[0])] * 2 + [np.cos(ang[1])] * 2, axis=1)
    s_up = np.concatenate([zero, np.sin(ang[0]), zero, np.sin(ang[1])], axis=1)
    s_dn = np.concatenate([-np.sin(ang[0]), zero, -np.sin(ang[1]), zero], axis=1)
    return tuple(jnp.asarray(t, dtype=F32) for t in (cos, s_up, s_dn))


def kernel(x, c, ctx, c_ctx, norm1_g, w_mod, b_mod, w_in, w_gate, b_gate, w_four_out, w_dw, b_dw,
           conv_ln_g, conv_ln_b, w_conv_out, lam_q1, lam_k1, lam_q2, lam_k2, subln_g, w_attn_out,
           w_o, norm2_g, w1, w3, w2, w_router, b_router, w1e, w3e, w2e, norm_f_g):
    n_batch, seq, d = x.shape
    ctx_len = ctx.shape[1]
    depth = w_in.shape[0]
    lat_rows, ctx_rows = n_batch * seq, n_batch * ctx_len
    all_rows = lat_rows + ctx_rows
    fdim = w_four_out.shape[1]
    cdim = w_conv_out.shape[1]
    assert n_batch < COND_ROWS and ctx_len % HALO_ROWS == 0 and seq % FFT_INNER == 0

    xa = (x.reshape(lat_rows, d), ctx.reshape(ctx_rows, d))
    cond = jnp.zeros((COND_ROWS, d), F32).at[:n_batch].set(c).at[n_batch].set(c_ctx)
    mod = ada_params_all(cond, w_mod, b_mod)
    rope = _rope_tables(seq)

    for l in range(depth):
        last = l == depth - 1
        rows = lat_rows if last else all_rows
        seqs = [seq] * n_batch + ([] if last else [ctx_len] * n_batch)
        lam_init = 0.8 - 0.6 * math.exp(-0.3 * l)
        lam_vecs = jnp.stack([lam_q1[l], lam_k1[l], lam_q2[l], lam_k2[l]]).astype(F32)

        h = norm_modulate(xa, norm1_g[l], mod[l], 0, all_rows, seq, n_batch)
        f = mm_plain(h, w_in, l, 0, fdim, rows, seq)
        cab = mm_plain(h, w_in, l, fdim, 2 * cdim, rows, seq)
        qkv = mm_qkv_rope(h, w_in, l, fdim + 2 * cdim, all_rows, lat_rows, seq, rope)
        four = fourier_latent(f, n_batch, seq)
        att = attention_latent(qkv, lam_vecs, subln_g[l], lam_init, n_batch, seq, ctx_len)
        if not last:
            four = jnp.concatenate([four, fourier_context(f, n_batch, ctx_len, lat_rows)], axis=0)
            att = jnp.concatenate([att, attention_context(qkv, lam_vecs, subln_g[l], lam_init,
                                                          n_batch, ctx_len, lat_rows)], axis=0)
        conv = conformer_conv(cab, w_dw[l], b_dw[l], conv_ln_g[l], conv_ln_b[l], rows, seqs)
        y = merge_branches(h, four, conv, att, w_gate, b_gate, w_four_out, w_conv_out,
                           w_attn_out, l, rows, seq)
        xa = mm_residual(y, w_o, l, xa, mod[l], 2, rows, seq, n_batch)

        i = l // 2
        if l % 2 == 0:
            h2 = norm_modulate(xa, norm2_g[l], mod[l], 3, rows, seq, n_batch)
            hm = mm_swiglu(h2, w1, w3, i, rows, seq)
            xa = mm_residual(hm, w2, i, xa, mod[l], 5, rows, seq, n_batch)
        else:
            h2p, route = norm_modulate(xa, norm2_g[l], mod[l], 3, rows, seq, n_batch,
                                       router=(w_router[i], b_router[i]))
            row_token, pos, tile_expert, n_used = moe_plan(route, w1e.shape[1])
            xs = moe_gather(h2p, row_token)
            hm = moe_grouped_swiglu(xs, w1e, w3e, i, tile_expert, n_used)
            ys = moe_grouped_down(hm, w2e, i, tile_expert, n_used)
            xa = moe_combine(ys, pos, xa, route, mod[l], 5, rows, seq, n_batch,
                             final_g=norm_f_g if last else None)
            if last:
                return xa.reshape(n_batch, seq, d)

    return final_norm(xa, norm_f_g, lat_rows).reshape(n_batch, seq, d)
```

```python
import functools
import math

import numpy as np
import jax
import jax.numpy as jnp
from jax import lax
from jax.experimental import pallas as pl
from jax.experimental.pallas import tpu as pltpu

F32 = jnp.float32
BF16 = jnp.bfloat16

GRID_W = 64
EPS = 1e-6
LN_EPS = 1e-5
N_MOD = 6
FOURIER_GROUPS = 4
CONV_WIDTH = 31
CONV_PAD = CONV_WIDTH // 2
ATTN_HEADS = 8
ATTN_HEAD_DIM = 128
ATTN_V_DIM = 2 * ATTN_HEAD_DIM
ATTN_QK_DIM = ATTN_HEADS * 2 * ATTN_HEAD_DIM
ATTN_DIM = ATTN_HEADS * ATTN_V_DIM
ROPE_BASE = 10000.0
ROPE_FREQS = ATTN_HEAD_DIM // 4
N_EXPERTS = 8
TOP_K = 2

LANES = 128
BF16_SUBLANES = 16
V7X_VMEM_LIMIT_BYTES = 56 * 1024 * 1024
FFT_INNER = 128
COND_ROWS = 8
HALO_ROWS = BF16_SUBLANES


def _tile(n, pref, align):
    t = (min(pref, n) // align) * align
    while t > align and n % t:
        t -= align
    assert t >= align and n % t == 0, (n, pref, align)
    return t


def _params(semantics):
    return pltpu.CompilerParams(dimension_semantics=semantics,
                                vmem_limit_bytes=V7X_VMEM_LIMIT_BYTES)


def _dot(a, b):
    return jnp.dot(a, b, preferred_element_type=F32)


def _ada_kernel(c_ref, w_ref, b_ref, o_ref):
    c = c_ref[...]
    s = (c * jax.nn.sigmoid(c)).astype(BF16)
    o_ref[0] = _dot(s, w_ref[0].astype(BF16)) + b_ref[0]


def ada_params_all(cond, w_mod, b_mod):
    depth, d, n = w_mod.shape
    bn = _tile(n, 512, LANES)
    return pl.pallas_call(
        _ada_kernel,
        out_shape=jax.ShapeDtypeStruct((depth, COND_ROWS, n), F32),
        grid=(depth, n // bn),
        in_specs=[pl.BlockSpec((COND_ROWS, d), lambda l, j: (0, 0)),
                  pl.BlockSpec((1, d, bn), lambda l, j: (l, 0, j)),
                  pl.BlockSpec((1, 1, bn), lambda l, j: (l, 0, j))],
        out_specs=pl.BlockSpec((1, COND_ROWS, bn), lambda l, j: (l, 0, j)),
        compiler_params=_params(("parallel", "parallel")),
        name="ada_params",
    )(cond, w_mod, b_mod.reshape(depth, 1, n))


def _mod_row(i, blocks_per_batch, n_batch):
    return jnp.minimum(i // blocks_per_batch, n_batch)


def _rms(x, g):
    ms = jnp.mean(x * x, axis=-1, keepdims=True)
    return x * lax.rsqrt(ms + EPS) * g


def _row_sources(src, lat_rows, bm):
    lat_blocks = lat_rows // bm
    if isinstance(src, tuple):
        lat, ctx = src
        off = 0
    else:
        lat = ctx = src
        off = lat_blocks if src.shape[0] > lat_rows else 0
    return (lat, ctx, lambda i: jnp.minimum(i, lat_blocks - 1),
            lambda i: off + jnp.maximum(i - lat_blocks, 0), lat_blocks)


def _pick_rows(i, lat_blocks, lat_ref, ctx_ref):
    return jnp.where(i < lat_blocks, lat_ref[...], ctx_ref[...])


def _normmod_kernel(xl_ref, xc_ref, g_ref, sh_ref, sc_ref, o_ref, *, blocks_per_batch, n_batch,
                    lat_blocks):
    i = pl.program_id(0)
    r = _mod_row(i, blocks_per_batch, n_batch)
    y = _rms(_pick_rows(i, lat_blocks, xl_ref, xc_ref), g_ref[...])
    h = y * (1.0 + sc_ref[pl.ds(r, 1), :]) + sh_ref[pl.ds(r, 1), :]
    o_ref[...] = h.astype(o_ref.dtype)


def _normmod_router_kernel(xl_ref, xc_ref, g_ref, sh_ref, sc_ref, wr_ref, br_ref, o_ref,
                           route_ref, *, blocks_per_batch, n_batch, lat_blocks):
    i = pl.program_id(0)
    r = _mod_row(i, blocks_per_batch, n_batch)
    y = _rms(_pick_rows(i, lat_blocks, xl_ref, xc_ref), g_ref[...])
    h = y * (1.0 + sc_ref[pl.ds(r, 1), :]) + sh_ref[pl.ds(r, 1), :]
    bits = lax.bitcast_convert_type(h.astype(BF16).astype(F32), jnp.uint32)
    half = bits.shape[1] // 2
    o_ref[...] = bits[:, :half] | (bits[:, half:] >> 16)
    logits = jnp.dot(h, wr_ref[...], preferred_element_type=F32,
                     precision=lax.Precision.HIGHEST) + br_ref[...]
    lane = lax.broadcasted_iota(jnp.int32, logits.shape, 1)
    v1 = jnp.max(logits, axis=-1, keepdims=True)
    i1 = jnp.min(jnp.where(logits == v1, lane, LANES), axis=-1, keepdims=True)
    rest = jnp.where(lane == i1, -jnp.inf, logits)
    v2 = jnp.max(rest, axis=-1, keepdims=True)
    i2 = jnp.min(jnp.where(rest == v2, lane, LANES), axis=-1, keepdims=True)
    e = jnp.exp(v2 - v1)
    w1 = 1.0 / (1.0 + e)
    route_ref[...] = (jnp.where(lane == 0, w1, 0.0) + jnp.where(lane == 1, e * w1, 0.0)
                      + jnp.where(lane == 2, i1.astype(F32), 0.0)
                      + jnp.where(lane == 3, i2.astype(F32), 0.0))


def _final_norm_kernel(x_ref, g_ref, o_ref):
    o_ref[...] = _rms(x_ref[...], g_ref[...])


def norm_modulate(xa, g, mod_l, which, rows, seq_rows, n_batch, router=None):
    d = mod_l.shape[1] // N_MOD
    bm = _tile(math.gcd(rows, seq_rows), 256, BF16_SUBLANES)
    x_lat, x_ctx, lat_idx, ctx_idx, lat_blocks = _row_sources(xa, n_batch * seq_rows, bm)
    kw = dict(blocks_per_batch=seq_rows // bm, n_batch=n_batch, lat_blocks=lat_blocks)
    in_specs = [pl.BlockSpec((bm, d), lambda i: (lat_idx(i), 0)),
                pl.BlockSpec((bm, d), lambda i: (ctx_idx(i), 0)),
                pl.BlockSpec((1, d), lambda i: (0, 0)),
                pl.BlockSpec((COND_ROWS, d), lambda i: (0, which)),
                pl.BlockSpec((COND_ROWS, d), lambda i: (0, which + 1))]
    args = [x_lat, x_ctx, g.reshape(1, d), mod_l, mod_l]
    h_shape = jax.ShapeDtypeStruct((rows, d), BF16)
    h_spec = pl.BlockSpec((bm, d), lambda i: (i, 0))
    if router is None:
        return pl.pallas_call(
            functools.partial(_normmod_kernel, **kw), out_shape=h_shape, grid=(rows // bm,),
            in_specs=in_specs, out_specs=h_spec, compiler_params=_params(("parallel",)),
            name="norm_modulate")(*args)
    w_r, b_r = router
    ne = w_r.shape[1]
    w_pad = jnp.zeros((d, LANES), F32).at[:, :ne].set(w_r)
    b_pad = jnp.full((1, LANES), -jnp.inf, F32).at[0, :ne].set(b_r)
    return pl.pallas_call(
        functools.partial(_normmod_router_kernel, **kw),
        out_shape=(jax.ShapeDtypeStruct((rows, d // 2), jnp.uint32),
                   jax.ShapeDtypeStruct((rows, LANES), F32)), grid=(rows // bm,),
        in_specs=in_specs + [pl.BlockSpec((d, LANES), lambda i: (0, 0)),
                             pl.BlockSpec((1, LANES), lambda i: (0, 0))],
        out_specs=(pl.BlockSpec((bm, d // 2), lambda i: (i, 0)),
                   pl.BlockSpec((bm, LANES), lambda i: (i, 0))),
        compiler_params=_params(("parallel",)), name="norm_modulate_router")(*args, w_pad, b_pad)


def final_norm(xa, g, rows):
    d = xa.shape[1]
    bm = _tile(rows, 256, 8)
    return pl.pallas_call(
        _final_norm_kernel, out_shape=jax.ShapeDtypeStruct((rows, d), F32), grid=(rows // bm,),
        in_specs=[pl.BlockSpec((bm, d), lambda i: (i, 0)), pl.BlockSpec((1, d), lambda i: (0, 0))],
        out_specs=pl.BlockSpec((bm, d), lambda i: (i, 0)),
        compiler_params=_params(("parallel",)), name="final_norm")(xa, g.reshape(1, d))


def _stage_weight(w_ref, wb_ref):
    @pl.when(pl.program_id(1) == 0)
    def _():
        wb_ref[...] = w_ref[0].astype(BF16)


def _wspec(layer, k, bn, col_blk_off=0, single_buffer=False):
    return pl.BlockSpec((1, k, bn), lambda j, i: (layer, 0, col_blk_off + j),
                        pipeline_mode=pl.Buffered(1) if single_buffer else None)


def _mm_plain_kernel(x_ref, w_ref, o_ref, wb_ref):
    _stage_weight(w_ref, wb_ref)
    o_ref[...] = _dot(x_ref[...], wb_ref[...]).astype(o_ref.dtype)


def _mm_rope_kernel(x_ref, w_ref, c_ref, s1_ref, s2_ref, o_ref, wb_ref, *, rope_cols):
    j = pl.program_id(0)
    _stage_weight(w_ref, wb_ref)

    @pl.when(j < rope_cols)
    def _():
        acc = _dot(x_ref[...], wb_ref[...])
        c, s1, s2 = c_ref[...], s1_ref[...], s2_ref[...]
        for g in range(acc.shape[1] // LANES):
            t = acc[:, g * LANES:(g + 1) * LANES]
            r = (t * c + pltpu.roll(t, ROPE_FREQS, 1) * s1
                 + pltpu.roll(t, LANES - ROPE_FREQS, 1) * s2)
            o_ref[:, g * LANES:(g + 1) * LANES] = r.astype(o_ref.dtype)

    @pl.when(j >= rope_cols)
    def _():
        o_ref[...] = _dot(x_ref[...], wb_ref[...]).astype(o_ref.dtype)


def _mm_res_kernel(x_ref, w_ref, resl_ref, resc_ref, gate_ref, o_ref, wb_ref, *,
                   blocks_per_batch, n_batch, lat_blocks):
    i = pl.program_id(1)
    r = _mod_row(i, blocks_per_batch, n_batch)
    _stage_weight(w_ref, wb_ref)
    res = _pick_rows(i, lat_blocks, resl_ref, resc_ref)
    o_ref[...] = res + gate_ref[pl.ds(r, 1), :] * _dot(x_ref[...], wb_ref[...])


def _mm_swiglu_kernel(x_ref, w1_ref, w3_ref, o_ref, w1b_ref, w3b_ref):
    _stage_weight(w1_ref, w1b_ref)
    _stage_weight(w3_ref, w3b_ref)
    x = x_ref[...]
    a = _dot(x, w1b_ref[...])
    o_ref[...] = (a * jax.nn.sigmoid(a) * _dot(x, w3b_ref[...])).astype(o_ref.dtype)


def _mm_tiles(rows, seq_rows, n, bn_pref=512):
    bm = _tile(math.gcd(rows, seq_rows), 512, BF16_SUBLANES)
    bn = _tile(n, bn_pref, LANES)
    return bm, bn


def mm_plain(x, w, layer, col_off, n, rows, seq_rows):
    k = x.shape[1]
    bm, bn = _mm_tiles(rows, seq_rows, math.gcd(n, col_off) if col_off else n, 1024)
    return pl.pallas_call(
        _mm_plain_kernel, out_shape=jax.ShapeDtypeStruct((rows, n), BF16),
        grid=(n // bn, rows // bm),
        in_specs=[pl.BlockSpec((bm, k), lambda j, i: (i, 0)), _wspec(layer, k, bn, col_off // bn)],
        out_specs=pl.BlockSpec((bm, bn), lambda j, i: (i, j)),
        scratch_shapes=[pltpu.VMEM((k, bn), BF16)],
        compiler_params=_params(("parallel", "arbitrary")), name="mm_plain")(x, w)


def mm_qkv_rope(x, w, layer, col_off, rows, lat_rows, seq, tables):
    k = x.shape[1]
    n = 2 * ATTN_QK_DIM + ATTN_DIM
    bm = _tile(math.gcd(rows, seq), 512, BF16_SUBLANES)
    bn = _tile(math.gcd(ATTN_QK_DIM, col_off), 1024, LANES)
    seq_blocks, lat_blocks, q_cols = seq // bm, lat_rows // bm, ATTN_QK_DIM // bn
    scale = ATTN_HEAD_DIM ** -0.5 * math.log2(math.e)
    cos, s_up, s_dn = tables
    zeros = jnp.zeros((bm, LANES), F32)
    cos_all = jnp.concatenate([cos * scale, cos, zeros + scale, zeros + 1.0], axis=0)
    sup_all = jnp.concatenate([s_up * scale, s_up, zeros, zeros], axis=0)
    sdn_all = jnp.concatenate([s_dn * scale, s_dn, zeros, zeros], axis=0)

    def tab_block(j, i):
        is_q = j < q_cols
        lat = jnp.where(is_q, 0, seq_blocks) + i % seq_blocks
        ctx = jnp.where(is_q, 2 * seq_blocks, 2 * seq_blocks + 1)
        return jnp.where(i < lat_blocks, lat, ctx), 0

    tab_spec = pl.BlockSpec((bm, LANES), tab_block)
    kern = functools.partial(_mm_rope_kernel, rope_cols=2 * q_cols)
    return pl.pallas_call(
        kern, out_shape=jax.ShapeDtypeStruct((rows, n), BF16), grid=(n // bn, rows // bm),
        in_specs=[pl.BlockSpec((bm, k), lambda j, i: (i, 0)),
                  _wspec(layer, k, bn, col_off // bn, single_buffer=True),
                  tab_spec, tab_spec, tab_spec],
        out_specs=pl.BlockSpec((bm, bn), lambda j, i: (i, j)),
        scratch_shapes=[pltpu.VMEM((k, bn), BF16)],
        compiler_params=_params(("parallel", "arbitrary")), name="mm_qkv_rope",
    )(x, w, cos_all, sup_all, sdn_all)


def mm_residual(x, w, layer, res, mod_l, which, rows, seq_rows, n_batch):
    _, k, n = w.shape
    bm, bn = _mm_tiles(rows, seq_rows, n, 1024)
    wide_bytes = k * bn * (4 + 2) + 2 * bm * k * 2 + 6 * bm * bn * 4
    if wide_bytes <= V7X_VMEM_LIMIT_BYTES * 7 // 8:
        wspec = _wspec(layer, k, bn, single_buffer=True)
    else:
        bm, bn = _mm_tiles(rows, seq_rows, n)
        wspec = _wspec(layer, k, bn)
    gate_blk = which * (n // bn)
    res_lat, res_ctx, lat_idx, ctx_idx, lat_blocks = _row_sources(res, n_batch * seq_rows, bm)
    kern = functools.partial(_mm_res_kernel, blocks_per_batch=seq_rows // bm, n_batch=n_batch,
                             lat_blocks=lat_blocks)
    return pl.pallas_call(
        kern, out_shape=jax.ShapeDtypeStruct((rows, n), F32), grid=(n // bn, rows // bm),
        in_specs=[pl.BlockSpec((bm, k), lambda j, i: (i, 0)), wspec,
                  pl.BlockSpec((bm, bn), lambda j, i: (lat_idx(i), j)),
                  pl.BlockSpec((bm, bn), lambda j, i: (ctx_idx(i), j)),
                  pl.BlockSpec((COND_ROWS, bn), lambda j, i: (0, gate_blk + j))],
        out_specs=pl.BlockSpec((bm, bn), lambda j, i: (i, j)),
        scratch_shapes=[pltpu.VMEM((k, bn), BF16)],
        compiler_params=_params(("parallel", "arbitrary")), name="mm_residual",
    )(x, w, res_lat, res_ctx, mod_l)


def mm_swiglu(x, w1, w3, layer, rows, seq_rows):
    _, k, n = w1.shape
    bm, bn = _mm_tiles(rows, seq_rows, n)
    return pl.pallas_call(
        _mm_swiglu_kernel, out_shape=jax.ShapeDtypeStruct((rows, n), BF16),
        grid=(n // bn, rows // bm),
        in_specs=[pl.BlockSpec((bm, k), lambda j, i: (i, 0)), _wspec(layer, k, bn),
                  _wspec(layer, k, bn)],
        out_specs=pl.BlockSpec((bm, bn), lambda j, i: (i, j)),
        scratch_shapes=[pltpu.VMEM((k, bn), BF16), pltpu.VMEM((k, bn), BF16)],
        compiler_params=_params(("parallel", "arbitrary")), name="mm_swiglu")(x, w1, w3)


MOE_TILE = 512
MOE_GATHER_ROWS = 256
MOE_COMBINE_ROWS = 128
DMA_LOOP_UNROLL = 8


def moe_plan(route, n_experts):
    n_tok = route.shape[0]
    n_pairs = TOP_K * n_tok
    e_pair = route[:, 2:2 + TOP_K].astype(jnp.int32).reshape(n_pairs)
    onehot = (e_pair[:, None] == jnp.arange(n_experts, dtype=jnp.int32)[None, :])
    csum = jnp.cumsum(onehot.astype(jnp.int32), axis=0)
    rank = jnp.take_along_axis(csum, e_pair[:, None], axis=1)[:, 0] - 1
    counts = csum[-1]
    tiles_per = (counts + MOE_TILE - 1) // MOE_TILE
    tile_end = jnp.cumsum(tiles_per)
    pos = (tile_end - tiles_per)[e_pair] * MOE_TILE + rank
    n_rows = n_pairs + n_experts * MOE_TILE
    n_tiles = n_rows // MOE_TILE
    row_token = jnp.zeros((n_rows,), jnp.int32).at[pos].set(
        jnp.arange(n_pairs, dtype=jnp.int32) // TOP_K)
    tile_expert = jnp.minimum(
        jnp.searchsorted(tile_end, jnp.arange(n_tiles, dtype=jnp.int32), side="right"),
        n_experts - 1).astype(jnp.int32)
    return row_token, pos.astype(jnp.int32), tile_expert, tile_end[-1:].astype(jnp.int32)


def _row_copy(src_hbm, row, dst, sem):
    return pltpu.make_async_copy(src_hbm.at[pl.ds(row, 1)], dst, sem)


def _moe_gather_kernel(tok_ref, h_hbm, o_ref, buf, sems):
    i = pl.program_id(0)
    n_blocks = pl.num_programs(0)
    rows = buf.shape[1]

    def issue_block(blk, slot):
        base = blk * rows

        def issue(r, carry):
            _row_copy(h_hbm, tok_ref[base + r], buf.at[slot, pl.ds(r, 1)], sems.at[slot]).start()
            return carry

        lax.fori_loop(0, rows, issue, 0, unroll=DMA_LOOP_UNROLL)

    @pl.when(i == 0)
    def _():
        issue_block(0, 0)

    @pl.when(i + 1 < n_blocks)
    def _():
        issue_block(i + 1, (i + 1) % 2)

    slot = i % 2

    def wait(r, carry):
        _row_copy(h_hbm, 0, buf.at[slot, pl.ds(r, 1)], sems.at[slot]).wait()
        return carry

    lax.fori_loop(0, rows, wait, 0, unroll=DMA_LOOP_UNROLL)
    u = buf[slot]
    half = u.shape[1]
    o_ref[:, :half] = lax.bitcast_convert_type(u & jnp.uint32(0xFFFF0000), F32).astype(BF16)
    o_ref[:, half:] = lax.bitcast_convert_type(u << 16, F32).astype(BF16)


def moe_gather(h_packed, row_token):
    n_rows = row_token.shape[0]
    half = h_packed.shape[1]
    rows = MOE_GATHER_ROWS
    return pl.pallas_call(
        _moe_gather_kernel, out_shape=jax.ShapeDtypeStruct((n_rows, 2 * half), BF16),
        grid_spec=pltpu.PrefetchScalarGridSpec(
            num_scalar_prefetch=1, grid=(n_rows // rows,),
            in_specs=[pl.BlockSpec(memory_space=pl.ANY)],
            out_specs=pl.BlockSpec((rows, 2 * half), lambda i, tok: (i, 0)),
            scratch_shapes=[pltpu.VMEM((2, rows, half), jnp.uint32),
                            pltpu.SemaphoreType.DMA((2,))]),
        compiler_params=_params(("arbitrary",)), name="moe_gather")(row_token, h_packed)


def _moe_stage_weight(te_ref, w_ref, wb_ref):
    t = pl.program_id(1)
    prev = te_ref[jnp.maximum(t - 1, 0)]

    @pl.when(jnp.logical_or(t == 0, te_ref[t] != prev))
    def _():
        wb_ref[...] = w_ref[0, 0].astype(BF16)


def _moe_swiglu_kernel(te_ref, nu_ref, x_ref, w1_ref, w3_ref, o_ref, w1b_ref, w3b_ref):
    _moe_stage_weight(te_ref, w1_ref, w1b_ref)
    _moe_stage_weight(te_ref, w3_ref, w3b_ref)
    used = pl.program_id(1) < nu_ref[0]

    @pl.when(used)
    def _():
        x = x_ref[...]
        a = _dot(x, w1b_ref[...])
        o_ref[...] = (a * jax.nn.sigmoid(a) * _dot(x, w3b_ref[...])).astype(o_ref.dtype)

    @pl.when(jnp.logical_not(used))
    def _():
        o_ref[...] = jnp.zeros(o_ref.shape, o_ref.dtype)


def _moe_down_kernel(te_ref, nu_ref, x_ref, w_ref, o_ref, wb_ref):
    _moe_stage_weight(te_ref, w_ref, wb_ref)
    used = pl.program_id(1) < nu_ref[0]

    @pl.when(used)
    def _():
        o_ref[...] = _dot(x_ref[...], wb_ref[...])

    @pl.when(jnp.logical_not(used))
    def _():
        o_ref[...] = jnp.zeros(o_ref.shape, o_ref.dtype)


def _moe_wspec(layer, k, bn):
    return pl.BlockSpec((1, 1, k, bn), lambda j, t, te, nu: (layer, te[t], 0, j))


def moe_grouped_swiglu(xs, w1e, w3e, layer, tile_expert, n_used):
    n_rows, k = xs.shape
    n = w1e.shape[3]
    bn = _tile(n, 512, LANES)
    return pl.pallas_call(
        _moe_swiglu_kernel, out_shape=jax.ShapeDtypeStruct((n_rows, n), BF16),
        grid_spec=pltpu.PrefetchScalarGridSpec(
            num_scalar_prefetch=2, grid=(n // bn, n_rows // MOE_TILE),
            in_specs=[pl.BlockSpec((MOE_TILE, k), lambda j, t, te, nu: (t, 0)),
                      _moe_wspec(layer, k, bn), _moe_wspec(layer, k, bn)],
            out_specs=pl.BlockSpec((MOE_TILE, bn), lambda j, t, te, nu: (t, j)),
            scratch_shapes=[pltpu.VMEM((k, bn), BF16), pltpu.VMEM((k, bn), BF16)]),
        compiler_params=_params(("parallel", "arbitrary")), name="moe_swiglu",
    )(tile_expert, n_used, xs, w1e, w3e)


def moe_grouped_down(hm, w2e, layer, tile_expert, n_used):
    n_rows, k = hm.shape
    n = w2e.shape[3]
    bn = _tile(n, 1024, LANES)
    return pl.pallas_call(
        _moe_down_kernel, out_shape=jax.ShapeDtypeStruct((n_rows, n), F32),
        grid_spec=pltpu.PrefetchScalarGridSpec(
            num_scalar_prefetch=2, grid=(n // bn, n_rows // MOE_TILE),
            in_specs=[pl.BlockSpec((MOE_TILE, k), lambda j, t, te, nu: (t, 0)),
                      _moe_wspec(layer, k, bn)],
            out_specs=pl.BlockSpec((MOE_TILE, bn), lambda j, t, te, nu: (t, j)),
            scratch_shapes=[pltpu.VMEM((k, bn), BF16)]),
        compiler_params=_params(("parallel", "arbitrary")), name="moe_down",
    )(tile_expert, n_used, hm, w2e)


def _moe_combine_kernel(pos_ref, y_hbm, res_ref, route_ref, gate_ref, fg_ref, o_ref, buf, sems, *,
                        blocks_per_batch, n_batch, final_norm_out):
    i = pl.program_id(0)
    n_blocks = pl.num_programs(0)
    rows = res_ref.shape[0]

    def issue_block(blk, slot):
        base = blk * rows * TOP_K

        def issue(r, carry):
            for s in range(TOP_K):
                _row_copy(y_hbm, pos_ref[base + TOP_K * r + s], buf.at[slot, s, pl.ds(r, 1)],
                          sems.at[slot]).start()
            return carry

        lax.fori_loop(0, rows, issue, 0, unroll=DMA_LOOP_UNROLL)

    @pl.when(i == 0)
    def _():
        issue_block(0, 0)

    @pl.when(i + 1 < n_blocks)
    def _():
        issue_block(i + 1, (i + 1) % 2)

    slot = i % 2

    def wait(r, carry):
        for s in range(TOP_K):
            _row_copy(y_hbm, 0, buf.at[slot, s, pl.ds(r, 1)], sems.at[slot]).wait()
        return carry

    lax.fori_loop(0, rows, wait, 0, unroll=DMA_LOOP_UNROLL)
    w = route_ref[...]
    y = w[:, 0:1] * buf[slot, 0] + w[:, 1:2] * buf[slot, 1]
    r = _mod_row(i, blocks_per_batch, n_batch)
    x_new = res_ref[...] + gate_ref[pl.ds(r, 1), :] * y
    o_ref[...] = _rms(x_new, fg_ref[...]) if final_norm_out else x_new


def moe_combine(ys, pos, res, route, mod_l, which, rows, seq_rows, n_batch, final_g=None):
    d = ys.shape[1]
    bm = _tile(math.gcd(rows, seq_rows), MOE_COMBINE_ROWS, 8)
    kern = functools.partial(_moe_combine_kernel, blocks_per_batch=seq_rows // bm,
                             n_batch=n_batch, final_norm_out=final_g is not None)
    fg = jnp.ones((1, d), F32) if final_g is None else final_g.reshape(1, d)
    return pl.pallas_call(
        kern, out_shape=jax.ShapeDtypeStruct((rows, d), F32),
        grid_spec=pltpu.PrefetchScalarGridSpec(
            num_scalar_prefetch=1, grid=(rows // bm,),
            in_specs=[pl.BlockSpec(memory_space=pl.ANY),
                      pl.BlockSpec((bm, d), lambda i, p: (i, 0)),
                      pl.BlockSpec((bm, LANES), lambda i, p: (i, 0)),
                      pl.BlockSpec((COND_ROWS, d), lambda i, p: (0, which)),
                      pl.BlockSpec((1, d), lambda i, p: (0, 0))],
            out_specs=pl.BlockSpec((bm, d), lambda i, p: (i, 0)),
            scratch_shapes=[pltpu.VMEM((2, TOP_K, bm, d), F32), pltpu.SemaphoreType.DMA((2,))]),
        compiler_params=_params(("arbitrary",)), name="moe_combine",
    )(pos, ys, res, route, mod_l, fg)


def _merge_kernel(h_ref, f_ref, c_ref, a_ref, wg0, wg1, wg2, bg0, bg1, bg2, wf, wc, wa, o_ref,
                  wg0b, wg1b, wg2b, wfb, wcb, wab):
    for w_ref, wb_ref in ((wg0, wg0b), (wg1, wg1b), (wg2, wg2b), (wf, wfb), (wc, wcb), (wa, wab)):
        _stage_weight(w_ref, wb_ref)
    h = h_ref[...]

    def branch(wg, bg, x_ref, w):
        gate = jax.nn.sigmoid(_dot(h, wg[...]) + bg[0])
        return gate * _dot(x_ref[...], w[...])

    y = (branch(wg0b, bg0, f_ref, wfb) + branch(wg1b, bg1, c_ref, wcb)
         + branch(wg2b, bg2, a_ref, wab))
    o_ref[...] = y.astype(o_ref.dtype)


def merge_branches(h, four, conv, att, w_gate, b_gate, w_four_out, w_conv_out, w_attn_out, layer,
                   rows, seq_rows):
    d = h.shape[1]
    bm = _tile(math.gcd(rows, seq_rows), 512, BF16_SUBLANES)
    bn = _tile(d, 256, LANES)
    nb = d // bn
    depth = b_gate.shape[0]
    b3 = b_gate.reshape(depth, 1, 3 * d)

    def row_spec(width):
        return pl.BlockSpec((bm, width), lambda j, i: (i, 0))

    def col_spec(kdim, blk_off):
        return pl.BlockSpec((1, kdim, bn), lambda j, i: (layer, 0, blk_off + j),
                            pipeline_mode=pl.Buffered(1))

    kdims = (d, d, d, four.shape[1], conv.shape[1], att.shape[1])
    in_specs = ([row_spec(d), row_spec(four.shape[1]), row_spec(conv.shape[1]),
                 row_spec(att.shape[1])]
                + [col_spec(d, r * nb) for r in range(3)]
                + [col_spec(1, r * nb) for r in range(3)]
                + [col_spec(kd, 0) for kd in kdims[3:]])
    return pl.pallas_call(
        _merge_kernel, out_shape=jax.ShapeDtypeStruct((rows, d), BF16), grid=(nb, rows // bm),
        in_specs=in_specs, out_specs=pl.BlockSpec((bm, bn), lambda j, i: (i, j)),
        scratch_shapes=[pltpu.VMEM((kd, bn), BF16) for kd in kdims],
        compiler_params=_params(("parallel", "arbitrary")), name="merge_branches",
    )(h, four, conv, att, w_gate, w_gate, w_gate, b3, b3, b3, w_four_out, w_conv_out, w_attn_out)


def _fourier_tables(n_pos, group_dim, n2):
    n1 = n_pos // n2
    a = np.arange(n1)
    ang1 = 2.0 * np.pi * np.outer(a, a) / n1
    w1 = np.concatenate([np.cos(ang1), -np.sin(ang1)], axis=0)
    k = (np.arange(n1)[:, None] + n1 * np.arange(n2)[None, :])[:, :, None]
    ang2 = 2.0 * np.pi * ((k * np.arange(n2)[None, None, :]) % n_pos) / n_pos
    gc, gs = np.cos(ang2), np.sin(ang2)
    g = np.concatenate([np.concatenate([gc, gs], axis=2),
                        np.concatenate([-gs, gc], axis=2)], axis=1)
    c = np.arange(group_dim)
    angc = 2.0 * np.pi * np.outer(c, c) / group_dim
    scale = 1.0 / math.sqrt(n_pos * group_dim)
    as_bf16 = lambda t: jnp.asarray(t, dtype=F32).astype(BF16)
    return as_bf16(w1), as_bf16(g), as_bf16(np.cos(angc)), as_bf16(np.sin(angc)), scale


def _four1_kernel(w_ref, x_ref, o_ref):
    o_ref[0] = _dot(w_ref[...], x_ref[...]).astype(o_ref.dtype)


def _four2_kernel(*refs, scale, group_dim, complex_in):
    if complex_in:
        g_ref, zr_ref, zi_ref, cc_ref, sc_ref, o_ref = refs
        z = jnp.concatenate([zr_ref[0], zi_ref[0]], axis=0)
        g = g_ref[0]
    else:
        g_ref, zr_ref, cc_ref, sc_ref, o_ref = refs
        z = zr_ref[...]
        g = g_ref[0][:, :z.shape[0]]
    p = _dot(g, z)
    half = p.shape[0] // 2
    pr, pi = p[:half].astype(BF16), p[half:].astype(BF16)
    cc, sc = cc_ref[...], sc_ref[...]
    for grp in range(o_ref.shape[-1] // group_dim):
        sl = slice(grp * group_dim, (grp + 1) * group_dim)
        o = _dot(pr[:, sl], cc) + _dot(pi[:, sl], sc)
        o_ref[:, sl] = (o * scale).astype(o_ref.dtype)


def fourier_latent(f, n_batch, seq):
    total_rows = n_batch * seq
    fdim = f.shape[1]
    gd = fdim // FOURIER_GROUPS
    n2 = FFT_INNER
    n1 = seq // n2
    w1, g, cc, sc, scale = _fourier_tables(seq, gd, n2)
    wide = n2 * fdim
    bn = _tile(wide, 8192, LANES)
    z = pl.pallas_call(
        _four1_kernel, out_shape=jax.ShapeDtypeStruct((n_batch, 2 * n1, wide), BF16),
        grid=(n_batch, wide // bn),
        in_specs=[pl.BlockSpec((2 * n1, n1), lambda b, j: (0, 0)),
                  pl.BlockSpec((n1, bn), lambda b, j: (b, j))],
        out_specs=pl.BlockSpec((1, 2 * n1, bn), lambda b, j: (b, 0, j)),
        compiler_params=_params(("parallel", "parallel")), name="fourier_stage1",
    )(w1, f.reshape(f.shape[0] // n2, wide))
    z = z.reshape(n_batch * 2 * n1, n2, fdim)
    kern = functools.partial(_four2_kernel, scale=scale, group_dim=gd, complex_in=True)
    out = pl.pallas_call(
        kern, out_shape=jax.ShapeDtypeStruct((total_rows // n1, n1 * fdim), BF16),
        grid=(n_batch, n1),
        in_specs=[pl.BlockSpec((1, 2 * n2, 2 * n2), lambda b, k: (k, 0, 0)),
                  pl.BlockSpec((1, n2, fdim), lambda b, k: (b * 2 * n1 + k, 0, 0)),
                  pl.BlockSpec((1, n2, fdim), lambda b, k: (b * 2 * n1 + n1 + k, 0, 0)),
                  pl.BlockSpec((gd, gd), lambda b, k: (0, 0)),
                  pl.BlockSpec((gd, gd), lambda b, k: (0, 0))],
        out_specs=pl.BlockSpec((n2, fdim), lambda b, k: (b, k)),
        compiler_params=_params(("parallel", "parallel")), name="fourier_stage2",
    )(g, z, z, cc, sc)
    return out.reshape(total_rows, fdim)


def fourier_context(f, n_batch, ctx_len, lat_rows):
    fdim = f.shape[1]
    gd = fdim // FOURIER_GROUPS
    n2 = ctx_len
    _, g, cc, sc, scale = _fourier_tables(ctx_len, gd, n2)
    first = lat_rows // ctx_len
    kern = functools.partial(_four2_kernel, scale=scale, group_dim=gd, complex_in=False)
    return pl.pallas_call(
        kern, out_shape=jax.ShapeDtypeStruct((n_batch * ctx_len, fdim), BF16), grid=(n_batch,),
        in_specs=[pl.BlockSpec((1, 2 * n2, 2 * n2), lambda b: (0, 0, 0)),
                  pl.BlockSpec((ctx_len, fdim), lambda b: (first + b, 0)),
                  pl.BlockSpec((gd, gd), lambda b: (0, 0)),
                  pl.BlockSpec((gd, gd), lambda b: (0, 0))],
        out_specs=pl.BlockSpec((ctx_len, fdim), lambda b: (b, 0)),
        compiler_params=_params(("parallel",)), name="fourier_context",
    )(g, f, cc, sc)


F32_SUBLANES = 8


def _conv_kernel(prev_ref, cur_ref, next_ref, w_ref, b_ref, g_ref, bb_ref, o_ref, z_ref, zs_ref,
                 *, starts, ends, chunk):
    i = pl.program_id(0)
    ts, c = o_ref.shape

    def glu(ref):
        v = ref[...].astype(F32)
        return v[:, :c] * jax.nn.sigmoid(v[:, c:])

    def any_eq(vals):
        hit = i == vals[0]
        for v in vals[1:]:
            hit = jnp.logical_or(hit, i == v)
        return hit

    z_ref[0:HALO_ROWS, :] = jnp.where(any_eq(starts), 0.0, glu(prev_ref))
    z_ref[HALO_ROWS:HALO_ROWS + ts, :] = glu(cur_ref)
    z_ref[HALO_ROWS + ts:, :] = jnp.where(any_eq(ends), 0.0, glu(next_ref))
    n_shift = zs_ref.shape[1]
    for b in range(1, F32_SUBLANES):
        zs_ref[b - 1] = z_ref[b:b + n_shift, :]
    bias, gamma, beta = b_ref[...], g_ref[...], bb_ref[...]
    for r0 in range(0, ts, chunk):
        acc = jnp.zeros((chunk, c), F32)
        for t in range(CONV_WIDTH):
            lo = r0 + t + HALO_ROWS - CONV_PAD
            b = lo % F32_SUBLANES
            src = z_ref if b == 0 else zs_ref.at[b - 1]
            acc = acc + w_ref[t:t + 1, :] * src[lo - b:lo - b + chunk, :]
        acc = acc + bias
        mu = jnp.mean(acc, axis=-1, keepdims=True)
        dev = acc - mu
        var = jnp.mean(dev * dev, axis=-1, keepdims=True)
        y = dev * lax.rsqrt(var + LN_EPS) * gamma + beta
        o_ref[r0:r0 + chunk, :] = (y * jax.nn.sigmoid(y)).astype(o_ref.dtype)


def conformer_conv(cab, w_dw, b_dw, ln_g, ln_b, rows, seq_lens):
    c = cab.shape[1] // 2
    ts = _tile(math.gcd(*seq_lens) if len(seq_lens) > 1 else seq_lens[0], 256, HALO_ROWS)
    per = ts // HALO_ROWS
    starts, ends, pos = [], [], 0
    for n in seq_lens:
        starts.append(pos // ts)
        pos += n
        ends.append(pos // ts - 1)
    assert pos == rows
    last_halo = rows // HALO_ROWS - 1
    kern = functools.partial(_conv_kernel, starts=tuple(starts), ends=tuple(ends), chunk=16)
    vec = lambda v: v.reshape(1, c)
    vspec = pl.BlockSpec((1, c), lambda i: (0, 0))
    return pl.pallas_call(
        kern, out_shape=jax.ShapeDtypeStruct((rows, c), BF16), grid=(rows // ts,),
        in_specs=[pl.BlockSpec((HALO_ROWS, 2 * c), lambda i: (jnp.maximum(i * per - 1, 0), 0)),
                  pl.BlockSpec((ts, 2 * c), lambda i: (i, 0)),
                  pl.BlockSpec((HALO_ROWS, 2 * c),
                               lambda i: (jnp.minimum((i + 1) * per, last_halo), 0)),
                  pl.BlockSpec((CONV_WIDTH, c), lambda i: (0, 0)), vspec, vspec, vspec],
        out_specs=pl.BlockSpec((ts, c), lambda i: (i, 0)),
        scratch_shapes=[pltpu.VMEM((ts + 2 * HALO_ROWS, c), F32),
                        pltpu.VMEM((F32_SUBLANES - 1, ts + 2 * HALO_ROWS - F32_SUBLANES, c), F32)],
        compiler_params=_params(("parallel",)), name="conformer_conv",
    )(cab, cab, cab, w_dw, vec(b_dw), vec(ln_g), vec(ln_b))


ATTN_ROW_GROUP = 64


def _lane_tile(x, reps):
    return x if reps == 1 else jnp.concatenate([x] * reps, axis=1)


def _attn_kernel(*refs, lam_init, tk, has_latent):
    if has_latent:
        (q_ref, k_ref, v_ref, kc_ref, vc_ref, lam_ref, g_ref, o_ref,
         m_ref, l_ref, a_ref, acc_ref) = refs[:12]
        slot_refs = refs[12:]
    else:
        (q_ref, kc_ref, vc_ref, lam_ref, g_ref, o_ref,
         m_ref, l_ref, a_ref, acc_ref) = refs[:10]
        slot_refs = refs[10:]
    s_ref, p_ref = slot_refs[:len(slot_refs) // 2], slot_refs[len(slot_refs) // 2:]
    hd = ATTN_HEAD_DIM
    m_ref[...] = jnp.full(m_ref.shape, -jnp.inf, F32)
    l_ref[...] = jnp.zeros(l_ref.shape, F32)
    acc_ref[...] = jnp.zeros(acc_ref.shape, F32)

    def scores(slot, k):
        n = k.shape[0]
        for mp in range(2):
            s_ref[slot][mp, :, :n] = lax.dot_general(
                q_ref[:, mp * hd:(mp + 1) * hd], k[:, mp * hd:(mp + 1) * hd],
                (((1,), (1,)), ((), ())), preferred_element_type=F32)

    def softmax_pv(slot, v):
        n = v.shape[0]
        tq = q_ref.shape[0]
        sub = min(ATTN_ROW_GROUP, tq)
        for mp in range(2):
            for r0 in range(0, tq, sub):
                rows = slice(r0, r0 + sub)
                s = s_ref[slot][mp, rows, :n]
                m_prev = m_ref[mp, rows]
                m_new = jnp.maximum(m_prev, jnp.max(s, axis=-1, keepdims=True))
                alpha = jnp.exp2(m_prev - m_new)
                p = jnp.exp2(s - _lane_tile(m_new, n // LANES))
                l_ref[mp, rows] = alpha * l_ref[mp, rows] + jnp.sum(p, axis=-1, keepdims=True)
                m_ref[mp, rows] = m_new
                a_ref[mp, rows] = alpha
                p_ref[slot][mp, rows, :n] = p.astype(BF16)
            acc_ref[mp] = (acc_ref[mp] * _lane_tile(a_ref[mp], ATTN_V_DIM // LANES)
                           + _dot(p_ref[slot][mp, :, :n], v))

    if has_latent:
        n_chunks = k_ref.shape[0] // tk
        assert n_chunks >= 2 and n_chunks % 2 == 0

        def chunk(ref, c):
            off = c * tk if isinstance(c, int) else pl.multiple_of(c * tk, tk)
            return ref[pl.ds(off, tk), :]

        scores(0, chunk(k_ref, 0))

        def body(i, carry):
            c = 2 * i
            scores(1, chunk(k_ref, c + 1))
            softmax_pv(0, chunk(v_ref, c))
            scores(0, chunk(k_ref, c + 2))
            softmax_pv(1, chunk(v_ref, c + 1))
            return carry

        for i in range(n_chunks // 2 - 1):
            body(i, 0)
        c = n_chunks - 2
        scores(1, chunk(k_ref, c + 1))
        softmax_pv(0, chunk(v_ref, c))
        scores(0, kc_ref[...])
        softmax_pv(1, chunk(v_ref, c + 1))
        softmax_pv(0, vc_ref[...])
    else:
        scores(0, kc_ref[...])
        softmax_pv(0, vc_ref[...])

    lam_v = lam_ref[...]
    lam = (jnp.exp(jnp.sum(lam_v[0:1] * lam_v[1:2], axis=-1, keepdims=True))
           - jnp.exp(jnp.sum(lam_v[2:3] * lam_v[3:4], axis=-1, keepdims=True)) + lam_init)
    reps = ATTN_V_DIM // LANES
    o = (acc_ref[0] * _lane_tile(1.0 / l_ref[0], reps)
         - lam * (acc_ref[1] * _lane_tile(1.0 / l_ref[1], reps)))
    o = _rms(o, g_ref[...]) * (1.0 - lam_init)
    o_ref[...] = o.astype(o_ref.dtype)


def _attn_scratch(tq, tk, slots):
    stat = pltpu.VMEM((2, tq, LANES), F32)
    return ([stat, stat, stat, pltpu.VMEM((2, tq, ATTN_V_DIM), F32)]
            + [pltpu.VMEM((2, tq, tk), F32)] * slots + [pltpu.VMEM((2, tq, tk), BF16)] * slots)


def attention_latent(qkv, lam_vecs, subln_g, lam_init, n_batch, seq, ctx_len):
    total_rows = n_batch * seq
    tq = _tile(seq, 512, BF16_SUBLANES)
    tk = _tile(seq // 2, 2048, LANES)
    qb = seq // tq
    kblk = ATTN_QK_DIM // ATTN_V_DIM
    vblk = 2 * ATTN_QK_DIM // ATTN_V_DIM
    first_ctx = n_batch * seq // ctx_len
    kern = functools.partial(_attn_kernel, lam_init=lam_init, tk=tk, has_latent=True)
    w = ATTN_V_DIM
    return pl.pallas_call(
        kern, out_shape=jax.ShapeDtypeStruct((total_rows, ATTN_DIM), BF16),
        grid=(n_batch, ATTN_HEADS, qb),
        in_specs=[pl.BlockSpec((tq, w), lambda b, h, i: (b * qb + i, h)),
                  pl.BlockSpec((seq, w), lambda b, h, i: (b, kblk + h)),
                  pl.BlockSpec((seq, w), lambda b, h, i: (b, vblk + h)),
                  pl.BlockSpec((ctx_len, w), lambda b, h, i: (first_ctx + b, kblk + h)),
                  pl.BlockSpec((ctx_len, w), lambda b, h, i: (first_ctx + b, vblk + h)),
                  pl.BlockSpec((4, ATTN_HEAD_DIM), lambda b, h, i: (0, 0)),
                  pl.BlockSpec((1, w), lambda b, h, i: (0, 0))],
        out_specs=pl.BlockSpec((tq, w), lambda b, h, i: (b * qb + i, h)),
        scratch_shapes=_attn_scratch(tq, max(tk, ctx_len), 2),
        compiler_params=_params(("parallel", "parallel", "parallel")), name="attention_latent",
    )(qkv, qkv, qkv, qkv, qkv, lam_vecs, subln_g.reshape(1, w))


def attention_context(qkv, lam_vecs, subln_g, lam_init, n_batch, ctx_len, lat_rows):
    kblk = ATTN_QK_DIM // ATTN_V_DIM
    vblk = 2 * ATTN_QK_DIM // ATTN_V_DIM
    first = lat_rows // ctx_len
    kern = functools.partial(_attn_kernel, lam_init=lam_init, tk=ctx_len, has_latent=False)
    w = ATTN_V_DIM
    return pl.pallas_call(
        kern, out_shape=jax.ShapeDtypeStruct((n_batch * ctx_len, ATTN_DIM), BF16),
        grid=(n_batch, ATTN_HEADS),
        in_specs=[pl.BlockSpec((ctx_len, w), lambda b, h: (first + b, h)),
                  pl.BlockSpec((ctx_len, w), lambda b, h: (first + b, kblk + h)),
                  pl.BlockSpec((ctx_len, w), lambda b, h: (first + b, vblk + h)),
                  pl.BlockSpec((4, ATTN_HEAD_DIM), lambda b, h: (0, 0)),
                  pl.BlockSpec((1, w), lambda b, h: (0, 0))],
        out_specs=pl.BlockSpec((ctx_len, w), lambda b, h: (b, h)),
        scratch_shapes=_attn_scratch(ctx_len, ctx_len, 1),
        compiler_params=_params(("parallel", "parallel")), name="attention_context",
    )(qkv, qkv, qkv, lam_vecs, subln_g.reshape(1, w))


def _rope_tables(n):
    pos = np.arange(n)
    inv = ROPE_BASE ** (-np.arange(ROPE_FREQS, dtype=np.float64) / ROPE_FREQS)
    ang = [(pos // GRID_W)[:, None] * inv, (pos % GRID_W)[:, None] * inv]
    zero = np.zeros((n, ROPE_FREQS))
    cos = np.concatenate([np.cos(ang[0])] * 2 + [np.cos(ang[1])] * 2, axis=1)
    s_up = np.concatenate([zero, np.sin(ang[0]), zero, np.sin(ang[1])], axis=1)
    s_dn = np.concatenate([-np.sin(ang[0]), zero, -np.sin(ang[1]), zero], axis=1)
    return tuple(jnp.asarray(t, dtype=F32) for t in (cos, s_up, s_dn))


def kernel(x, c, ctx, c_ctx, norm1_g, w_mod, b_mod, w_in, w_gate, b_gate, w_four_out, w_dw, b_dw,
           conv_ln_g, conv_ln_b, w_conv_out, lam_q1, lam_k1, lam_q2, lam_k2, subln_g, w_attn_out,
           w_o, norm2_g, w1, w3, w2, w_router, b_router, w1e, w3e, w2e, norm_f_g):
    n_batch, seq, d = x.shape
    ctx_len = ctx.shape[1]
    depth = w_in.shape[0]
    lat_rows, ctx_rows = n_batch * seq, n_batch * ctx_len
    all_rows = lat_rows + ctx_rows
    fdim = w_four_out.shape[1]
    cdim = w_conv_out.shape[1]
    assert n_batch < COND_ROWS and ctx_len % HALO_ROWS == 0 and seq % FFT_INNER == 0

    xa = (x.reshape(lat_rows, d), ctx.reshape(ctx_rows, d))
    cond = jnp.zeros((COND_ROWS, d), F32).at[:n_batch].set(c).at[n_batch].set(c_ctx)
    mod = ada_params_all(cond, w_mod, b_mod)
    rope = _rope_tables(seq)

    for l in range(depth):
        last = l == depth - 1
        rows = lat_rows if last else all_rows
        seqs = [seq] * n_batch + ([] if last else [ctx_len] * n_batch)
        lam_init = 0.8 - 0.6 * math.exp(-0.3 * l)
        lam_vecs = jnp.stack([lam_q1[l], lam_k1[l], lam_q2[l], lam_k2[l]]).astype(F32)

        h = norm_modulate(xa, norm1_g[l], mod[l], 0, all_rows, seq, n_batch)
        f = mm_plain(h, w_in, l, 0, fdim, rows, seq)
        cab = mm_plain(h, w_in, l, fdim, 2 * cdim, rows, seq)
        qkv = mm_qkv_rope(h, w_in, l, fdim + 2 * cdim, all_rows, lat_rows, seq, rope)
        four = fourier_latent(f, n_batch, seq)
        att = attention_latent(qkv, lam_vecs, subln_g[l], lam_init, n_batch, seq, ctx_len)
        if not last:
            four = jnp.concatenate([four, fourier_context(f, n_batch, ctx_len, lat_rows)], axis=0)
            att = jnp.concatenate([att, attention_context(qkv, lam_vecs, subln_g[l], lam_init,
                                                          n_batch, ctx_len, lat_rows)], axis=0)
        conv = conformer_conv(cab, w_dw[l], b_dw[l], conv_ln_g[l], conv_ln_b[l], rows, seqs)
        y = merge_branches(h, four, conv, att, w_gate, b_gate, w_four_out, w_conv_out,
                           w_attn_out, l, rows, seq)
        xa = mm_residual(y, w_o, l, xa, mod[l], 2, rows, seq, n_batch)

        i = l // 2
        if l % 2 == 0:
            h2 = norm_modulate(xa, norm2_g[l], mod[l], 3, rows, seq, n_batch)
            hm = mm_swiglu(h2, w1, w3, i, rows, seq)
            xa = mm_residual(hm, w2, i, xa, mod[l], 5, rows, seq, n_batch)
        else:
            h2p, route = norm_modulate(xa, norm2_g[l], mod[l], 3, rows, seq, n_batch,
                                       router=(w_router[i], b_router[i]))
            row_token, pos, tile_expert, n_used = moe_plan(route, w1e.shape[1])
            xs = moe_gather(h2p, row_token)
            hm = moe_grouped_swiglu(xs, w1e, w3e, i, tile_expert, n_used)
            ys = moe_grouped_down(hm, w2e, i, tile_expert, n_used)
            xa = moe_combine(ys, pos, xa, route, mod[l], 5, rows, seq, n_batch,
                             final_g=norm_f_g if last else None)
            if last:
                return xa.reshape(n_batch, seq, d)

    return final_norm(xa, norm_f_g, lat_rows).reshape(n_batch, seq, d)
```

```python
import functools
import math

import numpy as np
import jax
import jax.numpy as jnp
from jax import lax
from jax.experimental import pallas as pl
from jax.experimental.pallas import tpu as pltpu

F32 = jnp.float32
BF16 = jnp.bfloat16

GRID_W = 64
EPS = 1e-6
LN_EPS = 1e-5
N_MOD = 6
FOURIER_GROUPS = 4
CONV_WIDTH = 31
CONV_PAD = CONV_WIDTH // 2
ATTN_HEADS = 8
ATTN_HEAD_DIM = 128
ATTN_V_DIM = 2 * ATTN_HEAD_DIM
ATTN_QK_DIM = ATTN_HEADS * 2 * ATTN_HEAD_DIM
ATTN_DIM = ATTN_HEADS * ATTN_V_DIM
ROPE_BASE = 10000.0
ROPE_FREQS = ATTN_HEAD_DIM // 4
N_EXPERTS = 8
TOP_K = 2

LANES = 128
BF16_SUBLANES = 16
V7X_VMEM_LIMIT_BYTES = 56 * 1024 * 1024
FFT_INNER = 128
COND_ROWS = 8
HALO_ROWS = BF16_SUBLANES


def _tile(n, pref, align):
    t = (min(pref, n) // align) * align
    while t > align and n % t:
        t -= align
    assert t >= align and n % t == 0, (n, pref, align)
    return t


def _params(semantics):
    return pltpu.CompilerParams(dimension_semantics=semantics,
                                vmem_limit_bytes=V7X_VMEM_LIMIT_BYTES)


def _dot(a, b):
    return jnp.dot(a, b, preferred_element_type=F32)


def _ada_kernel(c_ref, w_ref, b_ref, o_ref):
    c = c_ref[...]
    s = (c * jax.nn.sigmoid(c)).astype(BF16)
    o_ref[0] = _dot(s, w_ref[0].astype(BF16)) + b_ref[0]


def ada_params_all(cond, w_mod, b_mod):
    depth, d, n = w_mod.shape
    bn = _tile(n, 512, LANES)
    return pl.pallas_call(
        _ada_kernel,
        out_shape=jax.ShapeDtypeStruct((depth, COND_ROWS, n), F32),
        grid=(depth, n // bn),
        in_specs=[pl.BlockSpec((COND_ROWS, d), lambda l, j: (0, 0)),
                  pl.BlockSpec((1, d, bn), lambda l, j: (l, 0, j)),
                  pl.BlockSpec((1, 1, bn), lambda l, j: (l, 0, j))],
        out_specs=pl.BlockSpec((1, COND_ROWS, bn), lambda l, j: (l, 0, j)),
        compiler_params=_params(("parallel", "parallel")),
        name="ada_params",
    )(cond, w_mod, b_mod.reshape(depth, 1, n))


def _mod_row(i, blocks_per_batch, n_batch):
    return jnp.minimum(i // blocks_per_batch, n_batch)


def _rms(x, g):
    ms = jnp.mean(x * x, axis=-1, keepdims=True)
    return x * lax.rsqrt(ms + EPS) * g


def _row_sources(src, lat_rows, bm):
    lat_blocks = lat_rows // bm
    if isinstance(src, tuple):
        lat, ctx = src
        off = 0
    else:
        lat = ctx = src
        off = lat_blocks if src.shape[0] > lat_rows else 0
    return (lat, ctx, lambda i: jnp.minimum(i, lat_blocks - 1),
            lambda i: off + jnp.maximum(i - lat_blocks, 0), lat_blocks)


def _pick_rows(i, lat_blocks, lat_ref, ctx_ref):
    return jnp.where(i < lat_blocks, lat_ref[...], ctx_ref[...])


def _normmod_kernel(xl_ref, xc_ref, g_ref, sh_ref, sc_ref, o_ref, *, blocks_per_batch, n_batch,
                    lat_blocks):
    i = pl.program_id(0)
    r = _mod_row(i, blocks_per_batch, n_batch)
    y = _rms(_pick_rows(i, lat_blocks, xl_ref, xc_ref), g_ref[...])
    h = y * (1.0 + sc_ref[pl.ds(r, 1), :]) + sh_ref[pl.ds(r, 1), :]
    o_ref[...] = h.astype(o_ref.dtype)


def _normmod_router_kernel(xl_ref, xc_ref, g_ref, sh_ref, sc_ref, wr_ref, br_ref, o_ref,
                           route_ref, *, blocks_per_batch, n_batch, lat_blocks):
    i = pl.program_id(0)
    r = _mod_row(i, blocks_per_batch, n_batch)
    y = _rms(_pick_rows(i, lat_blocks, xl_ref, xc_ref), g_ref[...])
    h = y * (1.0 + sc_ref[pl.ds(r, 1), :]) + sh_ref[pl.ds(r, 1), :]
    bits = lax.bitcast_convert_type(h.astype(BF16).astype(F32), jnp.uint32)
    half = bits.shape[1] // 2
    o_ref[...] = bits[:, :half] | (bits[:, half:] >> 16)
    logits = jnp.dot(h, wr_ref[...], preferred_element_type=F32,
                     precision=lax.Precision.HIGHEST) + br_ref[...]
    lane = lax.broadcasted_iota(jnp.int32, logits.shape, 1)
    v1 = jnp.max(logits, axis=-1, keepdims=True)
    i1 = jnp.min(jnp.where(logits == v1, lane, LANES), axis=-1, keepdims=True)
    rest = jnp.where(lane == i1, -jnp.inf, logits)
    v2 = jnp.max(rest, axis=-1, keepdims=True)
    i2 = jnp.min(jnp.where(rest == v2, lane, LANES), axis=-1, keepdims=True)
    e = jnp.exp(v2 - v1)
    w1 = 1.0 / (1.0 + e)
    route_ref[...] = (jnp.where(lane == 0, w1, 0.0) + jnp.where(lane == 1, e * w1, 0.0)
                      + jnp.where(lane == 2, i1.astype(F32), 0.0)
                      + jnp.where(lane == 3, i2.astype(F32), 0.0))


def _final_norm_kernel(x_ref, g_ref, o_ref):
    o_ref[...] = _rms(x_ref[...], g_ref[...])


def norm_modulate(xa, g, mod_l, which, rows, seq_rows, n_batch, router=None):
    d = mod_l.shape[1] // N_MOD
    bm = _tile(math.gcd(rows, seq_rows), 256, BF16_SUBLANES)
    x_lat, x_ctx, lat_idx, ctx_idx, lat_blocks = _row_sources(xa, n_batch * seq_rows, bm)
    kw = dict(blocks_per_batch=seq_rows // bm, n_batch=n_batch, lat_blocks=lat_blocks)
    in_specs = [pl.BlockSpec((bm, d), lambda i: (lat_idx(i), 0)),
                pl.BlockSpec((bm, d), lambda i: (ctx_idx(i), 0)),
                pl.BlockSpec((1, d), lambda i: (0, 0)),
                pl.BlockSpec((COND_ROWS, d), lambda i: (0, which)),
                pl.BlockSpec((COND_ROWS, d), lambda i: (0, which + 1))]
    args = [x_lat, x_ctx, g.reshape(1, d), mod_l, mod_l]
    h_shape = jax.ShapeDtypeStruct((rows, d), BF16)
    h_spec = pl.BlockSpec((bm, d), lambda i: (i, 0))
    if router is None:
        return pl.pallas_call(
            functools.partial(_normmod_kernel, **kw), out_shape=h_shape, grid=(rows // bm,),
            in_specs=in_specs, out_specs=h_spec, compiler_params=_params(("parallel",)),
            name="norm_modulate")(*args)
    w_r, b_r = router
    ne = w_r.shape[1]
    w_pad = jnp.zeros((d, LANES), F32).at[:, :ne].set(w_r)
    b_pad = jnp.full((1, LANES), -jnp.inf, F32).at[0, :ne].set(b_r)
    return pl.pallas_call(
        functools.partial(_normmod_router_kernel, **kw),
        out_shape=(jax.ShapeDtypeStruct((rows, d // 2), jnp.uint32),
                   jax.ShapeDtypeStruct((rows, LANES), F32)), grid=(rows // bm,),
        in_specs=in_specs + [pl.BlockSpec((d, LANES), lambda i: (0, 0)),
                             pl.BlockSpec((1, LANES), lambda i: (0, 0))],
        out_specs=(pl.BlockSpec((bm, d // 2), lambda i: (i, 0)),
                   pl.BlockSpec((bm, LANES), lambda i: (i, 0))),
        compiler_params=_params(("parallel",)), name="norm_modulate_router")(*args, w_pad, b_pad)


def final_norm(xa, g, rows):
    d = xa.shape[1]
    bm = _tile(rows, 256, 8)
    return pl.pallas_call(
        _final_norm_kernel, out_shape=jax.ShapeDtypeStruct((rows, d), F32), grid=(rows // bm,),
        in_specs=[pl.BlockSpec((bm, d), lambda i: (i, 0)), pl.BlockSpec((1, d), lambda i: (0, 0))],
        out_specs=pl.BlockSpec((bm, d), lambda i: (i, 0)),
        compiler_params=_params(("parallel",)), name="final_norm")(xa, g.reshape(1, d))


def _stage_weight(w_ref, wb_ref):
    @pl.when(pl.program_id(1) == 0)
    def _():
        wb_ref[...] = w_ref[0].astype(BF16)


def _wspec(layer, k, bn, col_blk_off=0, single_buffer=False):
    return pl.BlockSpec((1, k, bn), lambda j, i: (layer, 0, col_blk_off + j),
                        pipeline_mode=pl.Buffered(1) if single_buffer else None)


def _mm_plain_kernel(x_ref, w_ref, o_ref, wb_ref):
    _stage_weight(w_ref, wb_ref)
    o_ref[...] = _dot(x_ref[...], wb_ref[...]).astype(o_ref.dtype)


def _mm_rope_kernel(x_ref, w_ref, c_ref, s1_ref, s2_ref, o_ref, wb_ref, *, rope_cols):
    j = pl.program_id(0)
    _stage_weight(w_ref, wb_ref)

    @pl.when(j < rope_cols)
    def _():
        acc = _dot(x_ref[...], wb_ref[...])
        c, s1, s2 = c_ref[...], s1_ref[...], s2_ref[...]
        for g in range(acc.shape[1] // LANES):
            t = acc[:, g * LANES:(g + 1) * LANES]
            r = (t * c + pltpu.roll(t, ROPE_FREQS, 1) * s1
                 + pltpu.roll(t, LANES - ROPE_FREQS, 1) * s2)
            o_ref[:, g * LANES:(g + 1) * LANES] = r.astype(o_ref.dtype)

    @pl.when(j >= rope_cols)
    def _():
        o_ref[...] = _dot(x_ref[...], wb_ref[...]).astype(o_ref.dtype)


def _mm_res_kernel(x_ref, w_ref, resl_ref, resc_ref, gate_ref, o_ref, wb_ref, *,
                   blocks_per_batch, n_batch, lat_blocks):
    i = pl.program_id(1)
    r = _mod_row(i, blocks_per_batch, n_batch)
    _stage_weight(w_ref, wb_ref)
    res = _pick_rows(i, lat_blocks, resl_ref, resc_ref)
    o_ref[...] = res + gate_ref[pl.ds(r, 1), :] * _dot(x_ref[...], wb_ref[...])


def _mm_swiglu_kernel(x_ref, w1_ref, w3_ref, o_ref, w1b_ref, w3b_ref):
    _stage_weight(w1_ref, w1b_ref)
    _stage_weight(w3_ref, w3b_ref)
    x = x_ref[...]
    a = _dot(x, w1b_ref[...])
    o_ref[...] = (a * jax.nn.sigmoid(a) * _dot(x, w3b_ref[...])).astype(o_ref.dtype)


def _mm_tiles(rows, seq_rows, n, bn_pref=512):
    bm = _tile(math.gcd(rows, seq_rows), 512, BF16_SUBLANES)
    bn = _tile(n, bn_pref, LANES)
    return bm, bn


def mm_plain(x, w, layer, col_off, n, rows, seq_rows):
    k = x.shape[1]
    bm, bn = _mm_tiles(rows, seq_rows, math.gcd(n, col_off) if col_off else n, 1024)
    return pl.pallas_call(
        _mm_plain_kernel, out_shape=jax.ShapeDtypeStruct((rows, n), BF16),
        grid=(n // bn, rows // bm),
        in_specs=[pl.BlockSpec((bm, k), lambda j, i: (i, 0)), _wspec(layer, k, bn, col_off // bn)],
        out_specs=pl.BlockSpec((bm, bn), lambda j, i: (i, j)),
        scratch_shapes=[pltpu.VMEM((k, bn), BF16)],
        compiler_params=_params(("parallel", "arbitrary")), name="mm_plain")(x, w)


def mm_qkv_rope(x, w, layer, col_off, rows, lat_rows, seq, tables):
    k = x.shape[1]
    n = 2 * ATTN_QK_DIM + ATTN_DIM
    bm = _tile(math.gcd(rows, seq), 512, BF16_SUBLANES)
    bn = _tile(math.gcd(ATTN_QK_DIM, col_off), 1024, LANES)
    seq_blocks, lat_blocks, q_cols = seq // bm, lat_rows // bm, ATTN_QK_DIM // bn
    scale = ATTN_HEAD_DIM ** -0.5 * math.log2(math.e)
    cos, s_up, s_dn = tables
    zeros = jnp.zeros((bm, LANES), F32)
    cos_all = jnp.concatenate([cos * scale, cos, zeros + scale, zeros + 1.0], axis=0)
    sup_all = jnp.concatenate([s_up * scale, s_up, zeros, zeros], axis=0)
    sdn_all = jnp.concatenate([s_dn * scale, s_dn, zeros, zeros], axis=0)

    def tab_block(j, i):
        is_q = j < q_cols
        lat = jnp.where(is_q, 0, seq_blocks) + i % seq_blocks
        ctx = jnp.where(is_q, 2 * seq_blocks, 2 * seq_blocks + 1)
        return jnp.where(i < lat_blocks, lat, ctx), 0

    tab_spec = pl.BlockSpec((bm, LANES), tab_block)
    kern = functools.partial(_mm_rope_kernel, rope_cols=2 * q_cols)
    return pl.pallas_call(
        kern, out_shape=jax.ShapeDtypeStruct((rows, n), BF16), grid=(n // bn, rows // bm),
        in_specs=[pl.BlockSpec((bm, k), lambda j, i: (i, 0)),
                  _wspec(layer, k, bn, col_off // bn, single_buffer=True),
                  tab_spec, tab_spec, tab_spec],
        out_specs=pl.BlockSpec((bm, bn), lambda j, i: (i, j)),
        scratch_shapes=[pltpu.VMEM((k, bn), BF16)],
        compiler_params=_params(("parallel", "arbitrary")), name="mm_qkv_rope",
    )(x, w, cos_all, sup_all, sdn_all)


def mm_residual(x, w, layer, res, mod_l, which, rows, seq_rows, n_batch):
    _, k, n = w.shape
    bm, bn = _mm_tiles(rows, seq_rows, n, 1024)
    wide_bytes = k * bn * (4 + 2) + 2 * bm * k * 2 + 6 * bm * bn * 4
    if wide_bytes <= V7X_VMEM_LIMIT_BYTES * 7 // 8:
        wspec = _wspec(layer, k, bn, single_buffer=True)
    else:
        bm, bn = _mm_tiles(rows, seq_rows, n)
        wspec = _wspec(layer, k, bn)
    gate_blk = which * (n // bn)
    res_lat, res_ctx, lat_idx, ctx_idx, lat_blocks = _row_sources(res, n_batch * seq_rows, bm)
    kern = functools.partial(_mm_res_kernel, blocks_per_batch=seq_rows // bm, n_batch=n_batch,
                             lat_blocks=lat_blocks)
    return pl.pallas_call(
        kern, out_shape=jax.ShapeDtypeStruct((rows, n), F32), grid=(n // bn, rows // bm),
        in_specs=[pl.BlockSpec((bm, k), lambda j, i: (i, 0)), wspec,
                  pl.BlockSpec((bm, bn), lambda j, i: (lat_idx(i), j)),
                  pl.BlockSpec((bm, bn), lambda j, i: (ctx_idx(i), j)),
                  pl.BlockSpec((COND_ROWS, bn), lambda j, i: (0, gate_blk + j))],
        out_specs=pl.BlockSpec((bm, bn), lambda j, i: (i, j)),
        scratch_shapes=[pltpu.VMEM((k, bn), BF16)],
        compiler_params=_params(("parallel", "arbitrary")), name="mm_residual",
    )(x, w, res_lat, res_ctx, mod_l)


def mm_swiglu(x, w1, w3, layer, rows, seq_rows):
    _, k, n = w1.shape
    bm, bn = _mm_tiles(rows, seq_rows, n)
    return pl.pallas_call(
        _mm_swiglu_kernel, out_shape=jax.ShapeDtypeStruct((rows, n), BF16),
        grid=(n // bn, rows // bm),
        in_specs=[pl.BlockSpec((bm, k), lambda j, i: (i, 0)), _wspec(layer, k, bn),
                  _wspec(layer, k, bn)],
        out_specs=pl.BlockSpec((bm, bn), lambda j, i: (i, j)),
        scratch_shapes=[pltpu.VMEM((k, bn), BF16), pltpu.VMEM((k, bn), BF16)],
        compiler_params=_params(("parallel", "arbitrary")), name="mm_swiglu")(x, w1, w3)


MOE_TILE = 512
MOE_GATHER_ROWS = 256
MOE_COMBINE_ROWS = 128
DMA_LOOP_UNROLL = 8


def moe_plan(route, n_experts):
    n_tok = route.shape[0]
    n_pairs = TOP_K * n_tok
    e_pair = route[:, 2:2 + TOP_K].astype(jnp.int32).reshape(n_pairs)
    onehot = (e_pair[:, None] == jnp.arange(n_experts, dtype=jnp.int32)[None, :])
    csum = jnp.cumsum(onehot.astype(jnp.int32), axis=0)
    rank = jnp.take_along_axis(csum, e_pair[:, None], axis=1)[:, 0] - 1
    counts = csum[-1]
    tiles_per = (counts + MOE_TILE - 1) // MOE_TILE
    tile_end = jnp.cumsum(tiles_per)
    pos = (tile_end - tiles_per)[e_pair] * MOE_TILE + rank
    n_rows = n_pairs + n_experts * MOE_TILE
    n_tiles = n_rows // MOE_TILE
    row_token = jnp.zeros((n_rows,), jnp.int32).at[pos].set(
        jnp.arange(n_pairs, dtype=jnp.int32) // TOP_K)
    tile_expert = jnp.minimum(
        jnp.searchsorted(tile_end, jnp.arange(n_tiles, dtype=jnp.int32), side="right"),
        n_experts - 1).astype(jnp.int32)
    return row_token, pos.astype(jnp.int32), tile_expert, tile_end[-1:].astype(jnp.int32)


def _row_copy(src_hbm, row, dst, sem):
    return pltpu.make_async_copy(src_hbm.at[pl.ds(row, 1)], dst, sem)


def _moe_gather_kernel(tok_ref, h_hbm, o_ref, buf, sems):
    i = pl.program_id(0)
    n_blocks = pl.num_programs(0)
    rows = buf.shape[1]

    def issue_block(blk, slot):
        base = blk * rows

        def issue(rr, carry):
            for prio in range(2):
                r = 2 * rr + prio
                _row_copy(h_hbm, tok_ref[base + r], buf.at[slot, pl.ds(r, 1)],
                          sems.at[slot]).start(priority=prio)
            return carry

        lax.fori_loop(0, rows // 2, issue, 0, unroll=DMA_LOOP_UNROLL // 2)

    @pl.when(i == 0)
    def _():
        issue_block(0, 0)

    @pl.when(i + 1 < n_blocks)
    def _():
        issue_block(i + 1, (i + 1) % 2)

    slot = i % 2

    def wait(r, carry):
        _row_copy(h_hbm, 0, buf.at[slot, pl.ds(r, 1)], sems.at[slot]).wait()
        return carry

    lax.fori_loop(0, rows, wait, 0, unroll=DMA_LOOP_UNROLL)
    u = buf[slot]
    half = u.shape[1]
    o_ref[:, :half] = lax.bitcast_convert_type(u & jnp.uint32(0xFFFF0000), F32).astype(BF16)
    o_ref[:, half:] = lax.bitcast_convert_type(u << 16, F32).astype(BF16)


def moe_gather(h_packed, row_token):
    n_rows = row_token.shape[0]
    half = h_packed.shape[1]
    rows = MOE_GATHER_ROWS
    return pl.pallas_call(
        _moe_gather_kernel, out_shape=jax.ShapeDtypeStruct((n_rows, 2 * half), BF16),
        grid_spec=pltpu.PrefetchScalarGridSpec(
            num_scalar_prefetch=1, grid=(n_rows // rows,),
            in_specs=[pl.BlockSpec(memory_space=pl.ANY)],
            out_specs=pl.BlockSpec((rows, 2 * half), lambda i, tok: (i, 0)),
            scratch_shapes=[pltpu.VMEM((2, rows, half), jnp.uint32),
                            pltpu.SemaphoreType.DMA((2,))]),
        compiler_params=_params(("arbitrary",)), name="moe_gather")(row_token, h_packed)


def _moe_stage_weight(te_ref, w_ref, wb_ref):
    t = pl.program_id(1)
    prev = te_ref[jnp.maximum(t - 1, 0)]

    @pl.when(jnp.logical_or(t == 0, te_ref[t] != prev))
    def _():
        wb_ref[...] = w_ref[0, 0].astype(BF16)


def _moe_swiglu_kernel(te_ref, nu_ref, x_ref, w1_ref, w3_ref, o_ref, w1b_ref, w3b_ref):
    _moe_stage_weight(te_ref, w1_ref, w1b_ref)
    _moe_stage_weight(te_ref, w3_ref, w3b_ref)
    used = pl.program_id(1) < nu_ref[0]

    @pl.when(used)
    def _():
        x = x_ref[...]
        a = _dot(x, w1b_ref[...])
        o_ref[...] = (a * jax.nn.sigmoid(a) * _dot(x, w3b_ref[...])).astype(o_ref.dtype)

    @pl.when(jnp.logical_not(used))
    def _():
        o_ref[...] = jnp.zeros(o_ref.shape, o_ref.dtype)


def _moe_down_kernel(te_ref, nu_ref, x_ref, w_ref, o_ref, wb_ref):
    _moe_stage_weight(te_ref, w_ref, wb_ref)
    used = pl.program_id(1) < nu_ref[0]

    @pl.when(used)
    def _():
        o_ref[...] = _dot(x_ref[...], wb_ref[...])

    @pl.when(jnp.logical_not(used))
    def _():
        o_ref[...] = jnp.zeros(o_ref.shape, o_ref.dtype)


def _moe_wspec(layer, k, bn):
    return pl.BlockSpec((1, 1, k, bn), lambda j, t, te, nu: (layer, te[t], 0, j))


def moe_grouped_swiglu(xs, w1e, w3e, layer, tile_expert, n_used):
    n_rows, k = xs.shape
    n = w1e.shape[3]
    bn = _tile(n, 512, LANES)
    return pl.pallas_call(
        _moe_swiglu_kernel, out_shape=jax.ShapeDtypeStruct((n_rows, n), BF16),
        grid_spec=pltpu.PrefetchScalarGridSpec(
            num_scalar_prefetch=2, grid=(n // bn, n_rows // MOE_TILE),
            in_specs=[pl.BlockSpec((MOE_TILE, k), lambda j, t, te, nu: (t, 0)),
                      _moe_wspec(layer, k, bn), _moe_wspec(layer, k, bn)],
            out_specs=pl.BlockSpec((MOE_TILE, bn), lambda j, t, te, nu: (t, j)),
            scratch_shapes=[pltpu.VMEM((k, bn), BF16), pltpu.VMEM((k, bn), BF16)]),
        compiler_params=_params(("parallel", "arbitrary")), name="moe_swiglu",
    )(tile_expert, n_used, xs, w1e, w3e)


def moe_grouped_down(hm, w2e, layer, tile_expert, n_used):
    n_rows, k = hm.shape
    n = w2e.shape[3]
    bn = _tile(n, 1024, LANES)
    return pl.pallas_call(
        _moe_down_kernel, out_shape=jax.ShapeDtypeStruct((n_rows, n), F32),
        grid_spec=pltpu.PrefetchScalarGridSpec(
            num_scalar_prefetch=2, grid=(n // bn, n_rows // MOE_TILE),
            in_specs=[pl.BlockSpec((MOE_TILE, k), lambda j, t, te, nu: (t, 0)),
                      _moe_wspec(layer, k, bn)],
            out_specs=pl.BlockSpec((MOE_TILE, bn), lambda j, t, te, nu: (t, j)),
            scratch_shapes=[pltpu.VMEM((k, bn), BF16)]),
        compiler_params=_params(("parallel", "arbitrary")), name="moe_down",
    )(tile_expert, n_used, hm, w2e)


def _moe_combine_kernel(pos_ref, y_hbm, res_ref, route_ref, gate_ref, fg_ref, o_ref, buf, sems, *,
                        blocks_per_batch, n_batch, final_norm_out):
    i = pl.program_id(0)
    n_blocks = pl.num_programs(0)
    rows = res_ref.shape[0]

    def issue_block(blk, slot):
        base = blk * rows * TOP_K

        def issue(r, carry):
            for s in range(TOP_K):
                _row_copy(y_hbm, pos_ref[base + TOP_K * r + s], buf.at[slot, s, pl.ds(r, 1)],
                          sems.at[slot]).start(priority=s)
            return carry

        lax.fori_loop(0, rows, issue, 0, unroll=DMA_LOOP_UNROLL)

    @pl.when(i == 0)
    def _():
        issue_block(0, 0)

    @pl.when(i + 1 < n_blocks)
    def _():
        issue_block(i + 1, (i + 1) % 2)

    slot = i % 2

    def wait(r, carry):
        for s in range(TOP_K):
            _row_copy(y_hbm, 0, buf.at[slot, s, pl.ds(r, 1)], sems.at[slot]).wait()
        return carry

    lax.fori_loop(0, rows, wait, 0, unroll=DMA_LOOP_UNROLL)
    w = route_ref[...]
    y = w[:, 0:1] * buf[slot, 0] + w[:, 1:2] * buf[slot, 1]
    r = _mod_row(i, blocks_per_batch, n_batch)
    x_new = res_ref[...] + gate_ref[pl.ds(r, 1), :] * y
    o_ref[...] = _rms(x_new, fg_ref[...]) if final_norm_out else x_new


def moe_combine(ys, pos, res, route, mod_l, which, rows, seq_rows, n_batch, final_g=None):
    d = ys.shape[1]
    bm = _tile(math.gcd(rows, seq_rows), MOE_COMBINE_ROWS, 8)
    kern = functools.partial(_moe_combine_kernel, blocks_per_batch=seq_rows // bm,
                             n_batch=n_batch, final_norm_out=final_g is not None)
    fg = jnp.ones((1, d), F32) if final_g is None else final_g.reshape(1, d)
    return pl.pallas_call(
        kern, out_shape=jax.ShapeDtypeStruct((rows, d), F32),
        grid_spec=pltpu.PrefetchScalarGridSpec(
            num_scalar_prefetch=1, grid=(rows // bm,),
            in_specs=[pl.BlockSpec(memory_space=pl.ANY),
                      pl.BlockSpec((bm, d), lambda i, p: (i, 0)),
                      pl.BlockSpec((bm, LANES), lambda i, p: (i, 0)),
                      pl.BlockSpec((COND_ROWS, d), lambda i, p: (0, which)),
                      pl.BlockSpec((1, d), lambda i, p: (0, 0))],
            out_specs=pl.BlockSpec((bm, d), lambda i, p: (i, 0)),
            scratch_shapes=[pltpu.VMEM((2, TOP_K, bm, d), F32), pltpu.SemaphoreType.DMA((2,))]),
        compiler_params=_params(("arbitrary",)), name="moe_combine",
    )(pos, ys, res, route, mod_l, fg)


def _merge_kernel(h_ref, f_ref, c_ref, a_ref, wg0, wg1, wg2, bg0, bg1, bg2, wf, wc, wa, o_ref,
                  wg0b, wg1b, wg2b, wfb, wcb, wab):
    for w_ref, wb_ref in ((wg0, wg0b), (wg1, wg1b), (wg2, wg2b), (wf, wfb), (wc, wcb), (wa, wab)):
        _stage_weight(w_ref, wb_ref)
    h = h_ref[...]

    def branch(wg, bg, x_ref, w):
        gate = jax.nn.sigmoid(_dot(h, wg[...]) + bg[0])
        return gate * _dot(x_ref[...], w[...])

    y = (branch(wg0b, bg0, f_ref, wfb) + branch(wg1b, bg1, c_ref, wcb)
         + branch(wg2b, bg2, a_ref, wab))
    o_ref[...] = y.astype(o_ref.dtype)


def merge_branches(h, four, conv, att, w_gate, b_gate, w_four_out, w_conv_out, w_attn_out, layer,
                   rows, seq_rows):
    d = h.shape[1]
    bm = _tile(math.gcd(rows, seq_rows), 512, BF16_SUBLANES)
    bn = _tile(d, 256, LANES)
    nb = d // bn
    depth = b_gate.shape[0]
    b3 = b_gate.reshape(depth, 1, 3 * d)

    def row_spec(width):
        return pl.BlockSpec((bm, width), lambda j, i: (i, 0))

    def col_spec(kdim, blk_off):
        return pl.BlockSpec((1, kdim, bn), lambda j, i: (layer, 0, blk_off + j),
                            pipeline_mode=pl.Buffered(1))

    kdims = (d, d, d, four.shape[1], conv.shape[1], att.shape[1])
    in_specs = ([row_spec(d), row_spec(four.shape[1]), row_spec(conv.shape[1]),
                 row_spec(att.shape[1])]
                + [col_spec(d, r * nb) for r in range(3)]
                + [col_spec(1, r * nb) for r in range(3)]
                + [col_spec(kd, 0) for kd in kdims[3:]])
    return pl.pallas_call(
        _merge_kernel, out_shape=jax.ShapeDtypeStruct((rows, d), BF16), grid=(nb, rows // bm),
        in_specs=in_specs, out_specs=pl.BlockSpec((bm, bn), lambda j, i: (i, j)),
        scratch_shapes=[pltpu.VMEM((kd, bn), BF16) for kd in kdims],
        compiler_params=_params(("parallel", "arbitrary")), name="merge_branches",
    )(h, four, conv, att, w_gate, w_gate, w_gate, b3, b3, b3, w_four_out, w_conv_out, w_attn_out)


def _fourier_tables(n_pos, group_dim, n2):
    n1 = n_pos // n2
    a = np.arange(n1)
    ang1 = 2.0 * np.pi * np.outer(a, a) / n1
    w1 = np.concatenate([np.cos(ang1), -np.sin(ang1)], axis=0)
    k = (np.arange(n1)[:, None] + n1 * np.arange(n2)[None, :])[:, :, None]
    ang2 = 2.0 * np.pi * ((k * np.arange(n2)[None, None, :]) % n_pos) / n_pos
    gc, gs = np.cos(ang2), np.sin(ang2)
    g = np.concatenate([np.concatenate([gc, gs], axis=2),
                        np.concatenate([-gs, gc], axis=2)], axis=1)
    c = np.arange(group_dim)
    angc = 2.0 * np.pi * np.outer(c, c) / group_dim
    scale = 1.0 / math.sqrt(n_pos * group_dim)
    as_bf16 = lambda t: jnp.asarray(t, dtype=F32).astype(BF16)
    return as_bf16(w1), as_bf16(g), as_bf16(np.cos(angc)), as_bf16(np.sin(angc)), scale


def _four1_kernel(w_ref, x_ref, o_ref):
    o_ref[0] = _dot(w_ref[...], x_ref[...]).astype(o_ref.dtype)


def _four2_kernel(*refs, scale, group_dim, complex_in):
    if complex_in:
        g_ref, zr_ref, zi_ref, cc_ref, sc_ref, o_ref = refs
        z = jnp.concatenate([zr_ref[0], zi_ref[0]], axis=0)
        g = g_ref[0]
    else:
        g_ref, zr_ref, cc_ref, sc_ref, o_ref = refs
        z = zr_ref[...]
        g = g_ref[0][:, :z.shape[0]]
    p = _dot(g, z)
    half = p.shape[0] // 2
    pr, pi = p[:half].astype(BF16), p[half:].astype(BF16)
    cc, sc = cc_ref[...], sc_ref[...]
    for grp in range(o_ref.shape[-1] // group_dim):
        sl = slice(grp * group_dim, (grp + 1) * group_dim)
        o = _dot(pr[:, sl], cc) + _dot(pi[:, sl], sc)
        o_ref[:, sl] = (o * scale).astype(o_ref.dtype)


def fourier_latent(f, n_batch, seq):
    total_rows = n_batch * seq
    fdim = f.shape[1]
    gd = fdim // FOURIER_GROUPS
    n2 = FFT_INNER
    n1 = seq // n2
    w1, g, cc, sc, scale = _fourier_tables(seq, gd, n2)
    wide = n2 * fdim
    bn = _tile(wide, 8192, LANES)
    z = pl.pallas_call(
        _four1_kernel, out_shape=jax.ShapeDtypeStruct((n_batch, 2 * n1, wide), BF16),
        grid=(n_batch, wide // bn),
        in_specs=[pl.BlockSpec((2 * n1, n1), lambda b, j: (0, 0)),
                  pl.BlockSpec((n1, bn), lambda b, j: (b, j))],
        out_specs=pl.BlockSpec((1, 2 * n1, bn), lambda b, j: (b, 0, j)),
        compiler_params=_params(("parallel", "parallel")), name="fourier_stage1",
    )(w1, f.reshape(f.shape[0] // n2, wide))
    z = z.reshape(n_batch * 2 * n1, n2, fdim)
    kern = functools.partial(_four2_kernel, scale=scale, group_dim=gd, complex_in=True)
    out = pl.pallas_call(
        kern, out_shape=jax.ShapeDtypeStruct((total_rows // n1, n1 * fdim), BF16),
        grid=(n_batch, n1),
        in_specs=[pl.BlockSpec((1, 2 * n2, 2 * n2), lambda b, k: (k, 0, 0)),
                  pl.BlockSpec((1, n2, fdim), lambda b, k: (b * 2 * n1 + k, 0, 0)),
                  pl.BlockSpec((1, n2, fdim), lambda b, k: (b * 2 * n1 + n1 + k, 0, 0)),
                  pl.BlockSpec((gd, gd), lambda b, k: (0, 0)),
                  pl.BlockSpec((gd, gd), lambda b, k: (0, 0))],
        out_specs=pl.BlockSpec((n2, fdim), lambda b, k: (b, k)),
        compiler_params=_params(("parallel", "parallel")), name="fourier_stage2",
    )(g, z, z, cc, sc)
    return out.reshape(total_rows, fdim)


def fourier_context(f, n_batch, ctx_len, lat_rows):
    fdim = f.shape[1]
    gd = fdim // FOURIER_GROUPS
    n2 = ctx_len
    _, g, cc, sc, scale = _fourier_tables(ctx_len, gd, n2)
    first = lat_rows // ctx_len
    kern = functools.partial(_four2_kernel, scale=scale, group_dim=gd, complex_in=False)
    return pl.pallas_call(
        kern, out_shape=jax.ShapeDtypeStruct((n_batch * ctx_len, fdim), BF16), grid=(n_batch,),
        in_specs=[pl.BlockSpec((1, 2 * n2, 2 * n2), lambda b: (0, 0, 0)),
                  pl.BlockSpec((ctx_len, fdim), lambda b: (first + b, 0)),
                  pl.BlockSpec((gd, gd), lambda b: (0, 0)),
                  pl.BlockSpec((gd, gd), lambda b: (0, 0))],
        out_specs=pl.BlockSpec((ctx_len, fdim), lambda b: (b, 0)),
        compiler_params=_params(("parallel",)), name="fourier_context",
    )(g, f, cc, sc)


F32_SUBLANES = 8


def _conv_kernel(prev_ref, cur_ref, next_ref, w_ref, b_ref, g_ref, bb_ref, o_ref, z_ref, zs_ref,
                 *, starts, ends, chunk):
    i = pl.program_id(0)
    ts, c = o_ref.shape

    def glu(ref):
        v = ref[...].astype(F32)
        return v[:, :c] * jax.nn.sigmoid(v[:, c:])

    def any_eq(vals):
        hit = i == vals[0]
        for v in vals[1:]:
            hit = jnp.logical_or(hit, i == v)
        return hit

    z_ref[0:HALO_ROWS, :] = jnp.where(any_eq(starts), 0.0, glu(prev_ref))
    z_ref[HALO_ROWS:HALO_ROWS + ts, :] = glu(cur_ref)
    z_ref[HALO_ROWS + ts:, :] = jnp.where(any_eq(ends), 0.0, glu(next_ref))
    n_shift = zs_ref.shape[1]
    for b in range(1, F32_SUBLANES):
        zs_ref[b - 1] = z_ref[b:b + n_shift, :]
    bias, gamma, beta = b_ref[...], g_ref[...], bb_ref[...]
    for r0 in range(0, ts, chunk):
        acc = jnp.zeros((chunk, c), F32)
        for t in range(CONV_WIDTH):
            lo = r0 + t + HALO_ROWS - CONV_PAD
            b = lo % F32_SUBLANES
            src = z_ref if b == 0 else zs_ref.at[b - 1]
            acc = acc + w_ref[t:t + 1, :] * src[lo - b:lo - b + chunk, :]
        acc = acc + bias
        mu = jnp.mean(acc, axis=-1, keepdims=True)
        dev = acc - mu
        var = jnp.mean(dev * dev, axis=-1, keepdims=True)
        y = dev * lax.rsqrt(var + LN_EPS) * gamma + beta
        o_ref[r0:r0 + chunk, :] = (y * jax.nn.sigmoid(y)).astype(o_ref.dtype)


def conformer_conv(cab, w_dw, b_dw, ln_g, ln_b, rows, seq_lens):
    c = cab.shape[1] // 2
    ts = _tile(math.gcd(*seq_lens) if len(seq_lens) > 1 else seq_lens[0], 256, HALO_ROWS)
    per = ts // HALO_ROWS
    starts, ends, pos = [], [], 0
    for n in seq_lens:
        starts.append(pos // ts)
        pos += n
        ends.append(pos // ts - 1)
    assert pos == rows
    last_halo = rows // HALO_ROWS - 1
    kern = functools.partial(_conv_kernel, starts=tuple(starts), ends=tuple(ends), chunk=16)
    vec = lambda v: v.reshape(1, c)
    vspec = pl.BlockSpec((1, c), lambda i: (0, 0))
    return pl.pallas_call(
        kern, out_shape=jax.ShapeDtypeStruct((rows, c), BF16), grid=(rows // ts,),
        in_specs=[pl.BlockSpec((HALO_ROWS, 2 * c), lambda i: (jnp.maximum(i * per - 1, 0), 0)),
                  pl.BlockSpec((ts, 2 * c), lambda i: (i, 0)),
                  pl.BlockSpec((HALO_ROWS, 2 * c),
                               lambda i: (jnp.minimum((i + 1) * per, last_halo), 0)),
                  pl.BlockSpec((CONV_WIDTH, c), lambda i: (0, 0)), vspec, vspec, vspec],
        out_specs=pl.BlockSpec((ts, c), lambda i: (i, 0)),
        scratch_shapes=[pltpu.VMEM((ts + 2 * HALO_ROWS, c), F32),
                        pltpu.VMEM((F32_SUBLANES - 1, ts + 2 * HALO_ROWS - F32_SUBLANES, c), F32)],
        compiler_params=_params(("parallel",)), name="conformer_conv",
    )(cab, cab, cab, w_dw, vec(b_dw), vec(ln_g), vec(ln_b))


ATTN_ROW_GROUP = 64


def _lane_tile(x, reps):
    return x if reps == 1 else jnp.concatenate([x] * reps, axis=1)


def _attn_kernel(*refs, lam_init, tk, has_latent):
    if has_latent:
        (q_ref, k_ref, v_ref, kc_ref, vc_ref, lam_ref, g_ref, o_ref,
         m_ref, l_ref, a_ref, acc_ref) = refs[:12]
        slot_refs = refs[12:]
    else:
        (q_ref, kc_ref, vc_ref, lam_ref, g_ref, o_ref,
         m_ref, l_ref, a_ref, acc_ref) = refs[:10]
        slot_refs = refs[10:]
    s_ref, p_ref = slot_refs[:len(slot_refs) // 2], slot_refs[len(slot_refs) // 2:]
    hd = ATTN_HEAD_DIM
    m_ref[...] = jnp.full(m_ref.shape, -jnp.inf, F32)
    l_ref[...] = jnp.zeros(l_ref.shape, F32)
    acc_ref[...] = jnp.zeros(acc_ref.shape, F32)

    def scores(slot, k):
        n = k.shape[0]
        for mp in range(2):
            s_ref[slot][mp, :, :n] = lax.dot_general(
                q_ref[:, mp * hd:(mp + 1) * hd], k[:, mp * hd:(mp + 1) * hd],
                (((1,), (1,)), ((), ())), preferred_element_type=F32)

    def softmax_pv(slot, v):
        n = v.shape[0]
        tq = q_ref.shape[0]
        sub = min(ATTN_ROW_GROUP, tq)
        for mp in range(2):
            for r0 in range(0, tq, sub):
                rows = slice(r0, r0 + sub)
                s = s_ref[slot][mp, rows, :n]
                m_prev = m_ref[mp, rows]
                m_new = jnp.maximum(m_prev, jnp.max(s, axis=-1, keepdims=True))
                alpha = jnp.exp2(m_prev - m_new)
                p = jnp.exp2(s - _lane_tile(m_new, n // LANES))
                l_ref[mp, rows] = alpha * l_ref[mp, rows] + jnp.sum(p, axis=-1, keepdims=True)
                m_ref[mp, rows] = m_new
                a_ref[mp, rows] = alpha
                p_ref[slot][mp, rows, :n] = p.astype(BF16)
            acc_ref[mp] = (acc_ref[mp] * _lane_tile(a_ref[mp], ATTN_V_DIM // LANES)
                           + _dot(p_ref[slot][mp, :, :n], v))

    if has_latent:
        n_chunks = k_ref.shape[0] // tk
        assert n_chunks >= 2 and n_chunks % 2 == 0

        def chunk(ref, c):
            off = c * tk if isinstance(c, int) else pl.multiple_of(c * tk, tk)
            return ref[pl.ds(off, tk), :]

        scores(0, chunk(k_ref, 0))

        def body(i, carry):
            c = 2 * i
            scores(1, chunk(k_ref, c + 1))
            softmax_pv(0, chunk(v_ref, c))
            scores(0, chunk(k_ref, c + 2))
            softmax_pv(1, chunk(v_ref, c + 1))
            return carry

        for i in range(n_chunks // 2 - 1):
            body(i, 0)
        c = n_chunks - 2
        scores(1, chunk(k_ref, c + 1))
        softmax_pv(0, chunk(v_ref, c))
        scores(0, kc_ref[...])
        softmax_pv(1, chunk(v_ref, c + 1))
        softmax_pv(0, vc_ref[...])
    else:
        scores(0, kc_ref[...])
        softmax_pv(0, vc_ref[...])

    lam_v = lam_ref[...]
    lam = (jnp.exp(jnp.sum(lam_v[0:1] * lam_v[1:2], axis=-1, keepdims=True))
           - jnp.exp(jnp.sum(lam_v[2:3] * lam_v[3:4], axis=-1, keepdims=True)) + lam_init)
    reps = ATTN_V_DIM // LANES
    o = (acc_ref[0] * _lane_tile(1.0 / l_ref[0], reps)
         - lam * (acc_ref[1] * _lane_tile(1.0 / l_ref[1], reps)))
    o = _rms(o, g_ref[...]) * (1.0 - lam_init)
    o_ref[...] = o.astype(o_ref.dtype)


def _attn_scratch(tq, tk, slots):
    stat = pltpu.VMEM((2, tq, LANES), F32)
    return ([stat, stat, stat, pltpu.VMEM((2, tq, ATTN_V_DIM), F32)]
            + [pltpu.VMEM((2, tq, tk), F32)] * slots + [pltpu.VMEM((2, tq, tk), BF16)] * slots)


def attention_latent(qkv, lam_vecs, subln_g, lam_init, n_batch, seq, ctx_len):
    total_rows = n_batch * seq
    tq = _tile(seq, 512, BF16_SUBLANES)
    tk = _tile(seq // 2, 2048, LANES)
    qb = seq // tq
    kblk = ATTN_QK_DIM // ATTN_V_DIM
    vblk = 2 * ATTN_QK_DIM // ATTN_V_DIM
    first_ctx = n_batch * seq // ctx_len
    kern = functools.partial(_attn_kernel, lam_init=lam_init, tk=tk, has_latent=True)
    w = ATTN_V_DIM
    return pl.pallas_call(
        kern, out_shape=jax.ShapeDtypeStruct((total_rows, ATTN_DIM), BF16),
        grid=(n_batch, ATTN_HEADS, qb),
        in_specs=[pl.BlockSpec((tq, w), lambda b, h, i: (b * qb + i, h)),
                  pl.BlockSpec((seq, w), lambda b, h, i: (b, kblk + h)),
                  pl.BlockSpec((seq, w), lambda b, h, i: (b, vblk + h)),
                  pl.BlockSpec((ctx_len, w), lambda b, h, i: (first_ctx + b, kblk + h)),
                  pl.BlockSpec((ctx_len, w), lambda b, h, i: (first_ctx + b, vblk + h)),
                  pl.BlockSpec((4, ATTN_HEAD_DIM), lambda b, h, i: (0, 0)),
                  pl.BlockSpec((1, w), lambda b, h, i: (0, 0))],
        out_specs=pl.BlockSpec((tq, w), lambda b, h, i: (b * qb + i, h)),
        scratch_shapes=_attn_scratch(tq, max(tk, ctx_len), 2),
        compiler_params=_params(("parallel", "parallel", "parallel")), name="attention_latent",
    )(qkv, qkv, qkv, qkv, qkv, lam_vecs, subln_g.reshape(1, w))


def attention_context(qkv, lam_vecs, subln_g, lam_init, n_batch, ctx_len, lat_rows):
    kblk = ATTN_QK_DIM // ATTN_V_DIM
    vblk = 2 * ATTN_QK_DIM // ATTN_V_DIM
    first = lat_rows // ctx_len
    kern = functools.partial(_attn_kernel, lam_init=lam_init, tk=ctx_len, has_latent=False)
    w = ATTN_V_DIM
    return pl.pallas_call(
        kern, out_shape=jax.ShapeDtypeStruct((n_batch * ctx_len, ATTN_DIM), BF16),
        grid=(n_batch, ATTN_HEADS),
        in_specs=[pl.BlockSpec((ctx_len, w), lambda b, h: (first + b, h)),
                  pl.BlockSpec((ctx_len, w), lambda b, h: (first + b, kblk + h)),
                  pl.BlockSpec((ctx_len, w), lambda b, h: (first + b, vblk + h)),
                  pl.BlockSpec((4, ATTN_HEAD_DIM), lambda b, h: (0, 0)),
                  pl.BlockSpec((1, w), lambda b, h: (0, 0))],
        out_specs=pl.BlockSpec((ctx_len, w), lambda b, h: (b, h)),
        scratch_shapes=_attn_scratch(ctx_len, ctx_len, 1),
        compiler_params=_params(("parallel", "parallel")), name="attention_context",
    )(qkv, qkv, qkv, lam_vecs, subln_g.reshape(1, w))


def _rope_tables(n):
    pos = np.arange(n)
    inv = ROPE_BASE ** (-np.arange(ROPE_FREQS, dtype=np.float64) / ROPE_FREQS)
    ang = [(pos // GRID_W)[:, None] * inv, (pos % GRID_W)[:, None] * inv]
    zero = np.zeros((n, ROPE_FREQS))
    cos = np.concatenate([np.cos(ang[0])] * 2 + [np.cos(ang[1])] * 2, axis=1)
    s_up = np.concatenate([zero, np.sin(ang[0]), zero, np.sin(ang[1])], axis=1)
    s_dn = np.concatenate([-np.sin(ang[0]), zero, -np.sin(ang[1]), zero], axis=1)
    return tuple(jnp.asarray(t, dtype=F32) for t in (cos, s_up, s_dn))


def kernel(x, c, ctx, c_ctx, norm1_g, w_mod, b_mod, w_in, w_gate, b_gate, w_four_out, w_dw, b_dw,
           conv_ln_g, conv_ln_b, w_conv_out, lam_q1, lam_k1, lam_q2, lam_k2, subln_g, w_attn_out,
           w_o, norm2_g, w1, w3, w2, w_router, b_router, w1e, w3e, w2e, norm_f_g):
    n_batch, seq, d = x.shape
    ctx_len = ctx.shape[1]
    depth = w_in.shape[0]
    lat_rows, ctx_rows = n_batch * seq, n_batch * ctx_len
    all_rows = lat_rows + ctx_rows
    fdim = w_four_out.shape[1]
    cdim = w_conv_out.shape[1]
    assert n_batch < COND_ROWS and ctx_len % HALO_ROWS == 0 and seq % FFT_INNER == 0

    xa = (x.reshape(lat_rows, d), ctx.reshape(ctx_rows, d))
    cond = jnp.zeros((COND_ROWS, d), F32).at[:n_batch].set(c).at[n_batch].set(c_ctx)
    mod = ada_params_all(cond, w_mod, b_mod)
    rope = _rope_tables(seq)

    for l in range(depth):
        last = l == depth - 1
        rows = lat_rows if last else all_rows
        seqs = [seq] * n_batch + ([] if last else [ctx_len] * n_batch)
        lam_init = 0.8 - 0.6 * math.exp(-0.3 * l)
        lam_vecs = jnp.stack([lam_q1[l], lam_k1[l], lam_q2[l], lam_k2[l]]).astype(F32)

        h = norm_modulate(xa, norm1_g[l], mod[l], 0, all_rows, seq, n_batch)
        f = mm_plain(h, w_in, l, 0, fdim, rows, seq)
        cab = mm_plain(h, w_in, l, fdim, 2 * cdim, rows, seq)
        qkv = mm_qkv_rope(h, w_in, l, fdim + 2 * cdim, all_rows, lat_rows, seq, rope)
        four = fourier_latent(f, n_batch, seq)
        att = attention_latent(qkv, lam_vecs, subln_g[l], lam_init, n_batch, seq, ctx_len)
        if not last:
            four = jnp.concatenate([four, fourier_context(f, n_batch, ctx_len, lat_rows)], axis=0)
            att = jnp.concatenate([att, attention_context(qkv, lam_vecs, subln_g[l], lam_init,
                                                          n_batch, ctx_len, lat_rows)], axis=0)
        conv = conformer_conv(cab, w_dw[l], b_dw[l], conv_ln_g[l], conv_ln_b[l], rows, seqs)
        y = merge_branches(h, four, conv, att, w_gate, b_gate, w_four_out, w_conv_out,
                           w_attn_out, l, rows, seq)
        xa = mm_residual(y, w_o, l, xa, mod[l], 2, rows, seq, n_batch)

        i = l // 2
        if l % 2 == 0:
            h2 = norm_modulate(xa, norm2_g[l], mod[l], 3, rows, seq, n_batch)
            hm = mm_swiglu(h2, w1, w3, i, rows, seq)
            xa = mm_residual(hm, w2, i, xa, mod[l], 5, rows, seq, n_batch)
        else:
            h2p, route = norm_modulate(xa, norm2_g[l], mod[l], 3, rows, seq, n_batch,
                                       router=(w_router[i], b_router[i]))
            row_token, pos, tile_expert, n_used = moe_plan(route, w1e.shape[1])
            xs = moe_gather(h2p, row_token)
            hm = moe_grouped_swiglu(xs, w1e, w3e, i, tile_expert, n_used)
            ys = moe_grouped_down(hm, w2e, i, tile_expert, n_used)
            xa = moe_combine(ys, pos, xa, route, mod[l], 5, rows, seq, n_batch,
                             final_g=norm_f_g if last else None)
            if last:
                return xa.reshape(n_batch, seq, d)

    return final_norm(xa, norm_f_g, lat_rows).reshape(n_batch, seq, d)
```
